```python
import jax, jax.numpy as jnp
from jax import lax
import numpy as np

D_MODEL = 1024
BATCH = 16
SEQ = 4096
DEPTH = 1

GRID_W = 64
NA_WIDTH = D_MODEL // 2
NA_HEADS = 8
NA_HEAD_DIM = NA_WIDTH // NA_HEADS
WIN_ROWS_MAX = 8
WIN_COLS = 16
POOL_WIDTH = D_MODEL - NA_WIDTH
POOL_WINDOWS = (2, 4, 8, 16)
POOL_GROUPS = len(POOL_WINDOWS)
POOL_GROUP_DIM = POOL_WIDTH // POOL_GROUPS
MIX_WIDTH = NA_WIDTH + POOL_WIDTH
IN_WIDTH = 3 * NA_WIDTH + POOL_WIDTH
D_FF = -(-8 * D_MODEL // (3 * 256)) * 256
EPS = 1e-6

kernel_name = "hybrid_neighbourhood_attn_multiscale_pool_block"


def rms_norm(x, g):
    xf = x.astype(jnp.float32)
    y = xf * lax.rsqrt(jnp.mean(xf * xf, axis=-1, keepdims=True) + EPS)
    return (y * g.astype(jnp.float32)).astype(x.dtype)


def neighbourhood_attention(q, k, v, rpb):
    B, T, H, Dh = q.shape
    rows = T // GRID_W
    wr = min(WIN_ROWS_MAX, rows)
    q = q.reshape(B, rows, GRID_W, H, Dh)
    k = k.reshape(B, rows, GRID_W, H, Dh)
    v = v.reshape(B, rows, GRID_W, H, Dh)
    row_start = jnp.clip(jnp.arange(rows) - wr // 2, 0, rows - wr)
    cols = jnp.arange(GRID_W)
    col_idx = jnp.clip(cols - WIN_COLS // 2, 0, GRID_W - WIN_COLS)[:, None] + jnp.arange(WIN_COLS)
    col_rel = col_idx - cols[:, None] + (WIN_COLS - 1)
    rpb_cols = rpb.astype(jnp.float32)[:, :, col_rel]
    scale = Dh ** -0.5

    def one_row(r):
        rs = row_start[r]
        kb = lax.dynamic_slice_in_dim(k, rs, wr, axis=1)
        vb = lax.dynamic_slice_in_dim(v, rs, wr, axis=1)
        kg = kb[:, :, col_idx]
        vg = vb[:, :, col_idx]
        qr = lax.dynamic_index_in_dim(q, r, axis=1, keepdims=False)
        row_rel = rs + jnp.arange(wr) - r + (WIN_ROWS_MAX - 1)
        bias = jnp.transpose(rpb_cols[:, row_rel], (0, 2, 1, 3))
        s = jnp.einsum('bqhd,bwqkhd->bhqwk', qr, kg,
                       preferred_element_type=jnp.float32) * scale + bias[None]
        p = jax.nn.softmax(s.reshape(B, H, GRID_W, wr * WIN_COLS), axis=-1)
        p = p.reshape(B, H, GRID_W, wr, WIN_COLS).astype(v.dtype)
        return jnp.einsum('bhqwk,bwqkhd->bqhd', p, vg)

    out = lax.map(one_row, jnp.arange(rows))
    return jnp.moveaxis(out, 0, 1).reshape(B, T, H * Dh)


def multiscale_pool(u, w_pool, pool_scale):
    B, T, C = u.shape
    uf = u.astype(jnp.float32)
    csum = jnp.concatenate([jnp.zeros((B, 1, C), jnp.float32), jnp.cumsum(uf, axis=1)], axis=1)
    t = jnp.arange(T)
    outs = []
    for g, w in enumerate(POOL_WINDOWS):
        sl = slice(g * POOL_GROUP_DIM, (g + 1) * POOL_GROUP_DIM)
        lo = jnp.clip(t - w // 2, 0, T)
        hi = jnp.clip(t + w // 2, 0, T)
        cg = csum[:, :, sl]
        mean = (cg[:, hi] - cg[:, lo]) / (hi - lo).astype(jnp.float32)[:, None]
        outs.append(mean - uf[:, :, sl])
    d = jnp.stack(outs, axis=2).astype(u.dtype)
    y = jnp.einsum('btgc,gcd->btgd', d, w_pool).reshape(B, T, C)
    return y * pool_scale


def setup_inputs(seed: int = 0) -> dict:
    key = jax.random.key(seed)
    ks = jax.random.split(key, 14)
    f32 = jnp.float32
    nrm = lambda k, shape, fan_in: jax.random.normal(k, shape, f32) * fan_in ** -0.5
    gain = lambda k, shape: 1.0 + 0.02 * jax.random.normal(k, shape, f32)
    return {
        "x": jax.random.normal(ks[0], (BATCH, SEQ, D_MODEL), f32),
        "norm1_g": gain(ks[1], (DEPTH, D_MODEL)),
        "w_in": nrm(ks[2], (DEPTH, D_MODEL, IN_WIDTH), D_MODEL),
        "q_norm_g": gain(ks[3], (DEPTH, NA_HEAD_DIM)),
        "k_norm_g": gain(ks[4], (DEPTH, NA_HEAD_DIM)),
        "rpb": 0.1 * jax.random.normal(ks[5], (DEPTH, NA_HEADS, 2 * WIN_ROWS_MAX - 1, 2 * WIN_COLS - 1), f32),
        "w_pool": nrm(ks[6], (DEPTH, POOL_GROUPS, POOL_GROUP_DIM, POOL_GROUP_DIM), POOL_GROUP_DIM),
        "pool_scale": gain(ks[7], (DEPTH, POOL_WIDTH)),
        "w_out": nrm(ks[8], (DEPTH, MIX_WIDTH, D_MODEL), MIX_WIDTH),
        "norm2_g": gain(ks[9], (DEPTH, D_MODEL)),
        "w_gate": nrm(ks[10], (DEPTH, D_MODEL, D_FF), D_MODEL),
        "w_up": nrm(ks[11], (DEPTH, D_MODEL, D_FF), D_MODEL),
        "w_down": nrm(ks[12], (DEPTH, D_FF, D_MODEL), D_FF),
    }


def reference(x, norm1_g, w_in, q_norm_g, k_norm_g, rpb, w_pool, pool_scale, w_out,
              norm2_g, w_gate, w_up, w_down):
    B, T, _ = x.shape
    for l in range(DEPTH):
        h = rms_norm(x, norm1_g[l])
        proj = h @ w_in[l]
        q = proj[..., :NA_WIDTH].reshape(B, T, NA_HEADS, NA_HEAD_DIM)
        k = proj[..., NA_WIDTH:2 * NA_WIDTH].reshape(B, T, NA_HEADS, NA_HEAD_DIM)
        v = proj[..., 2 * NA_WIDTH:3 * NA_WIDTH].reshape(B, T, NA_HEADS, NA_HEAD_DIM)
        u = proj[..., 3 * NA_WIDTH:]
        q = rms_norm(q, q_norm_g[l])
        k = rms_norm(k, k_norm_g[l])
        a = neighbourhood_attention(q, k, v, rpb[l])
        p = multiscale_pool(u, w_pool[l], pool_scale[l])
        x = x + jnp.concatenate([a, p], axis=-1) @ w_out[l]
        h2 = rms_norm(x, norm2_g[l])
        x = x + (jax.nn.silu(h2 @ w_gate[l]) * (h2 @ w_up[l])) @ w_down[l]
    return x
```

```python
import functools

import jax
import jax.numpy as jnp
import numpy as np
from jax import lax
from jax.experimental import pallas as pl
from jax.experimental.pallas import tpu as pltpu

D_MODEL = 1024
GRID_W = 64
NA_WIDTH = D_MODEL // 2
NA_HEADS = 8
NA_HEAD_DIM = NA_WIDTH // NA_HEADS
WIN_ROWS_MAX = 8
WIN_COLS = 16
POOL_WIDTH = D_MODEL - NA_WIDTH
POOL_WINDOWS = (2, 4, 8, 16)
POOL_GROUPS = len(POOL_WINDOWS)
POOL_GROUP_DIM = POOL_WIDTH // POOL_GROUPS
IN_WIDTH = 3 * NA_WIDTH + POOL_WIDTH
EPS = 1e-6

LANES = 128
HEAD_PAIRS = NA_WIDTH // LANES
MASK_VALUE = -1e30
POOL_HALO = 16
VMEM_LIMIT_BYTES = 56 * 1024 * 1024

PROJ_BLOCK = 512
MIX_ROWS = 8
FFN_BLOCK = 512

F32 = jnp.float32
BF16 = jnp.bfloat16


def _rms_scale(x):
    return lax.rsqrt(jnp.mean(x * x, axis=-1, keepdims=True) + EPS)


def _proj_kernel(x_ref, g1_ref, w_ref, gq_ref, gk_ref, seg_ref,
                 q_ref, k_ref, v_ref, u_ref):
    x = x_ref[...]
    h = (x * _rms_scale(x) * g1_ref[...]).astype(BF16)
    proj = jnp.dot(h, w_ref[...], preferred_element_type=F32)

    def head_norm(t, g):
        ms = jnp.dot((t * t).astype(BF16), seg_ref[...], preferred_element_type=F32)
        return t * lax.rsqrt(ms + EPS) * g

    q = head_norm(proj[:, :NA_WIDTH], gq_ref[...])
    k = head_norm(proj[:, NA_WIDTH:2 * NA_WIDTH], gk_ref[...])
    q_ref[...] = (q * (NA_HEAD_DIM ** -0.5)).astype(BF16)
    k_ref[...] = k.astype(BF16)
    v_ref[...] = proj[:, 2 * NA_WIDTH:3 * NA_WIDTH].astype(BF16)
    u_ref[...] = proj[:, 3 * NA_WIDTH:]


def _proj_call(x2, g1, w_in, gq, gk, seg):
    n = x2.shape[0]
    tm = PROJ_BLOCK
    const = lambda i: (0, 0)
    row = lambda i: (i, 0)
    return pl.pallas_call(
        _proj_kernel,
        grid=(n // tm,),
        in_specs=[
            pl.BlockSpec((tm, D_MODEL), row),
            pl.BlockSpec((1, D_MODEL), const),
            pl.BlockSpec((D_MODEL, IN_WIDTH), const),
            pl.BlockSpec((1, NA_WIDTH), const),
            pl.BlockSpec((1, NA_WIDTH), const),
            pl.BlockSpec((NA_WIDTH, NA_WIDTH), const),
        ],
        out_specs=[
            pl.BlockSpec((tm, NA_WIDTH), row),
            pl.BlockSpec((tm, NA_WIDTH), row),
            pl.BlockSpec((tm, NA_WIDTH), row),
            pl.BlockSpec((tm, POOL_WIDTH), row),
        ],
        out_shape=[
            jax.ShapeDtypeStruct((n, NA_WIDTH), BF16),
            jax.ShapeDtypeStruct((n, NA_WIDTH), BF16),
            jax.ShapeDtypeStruct((n, NA_WIDTH), BF16),
            jax.ShapeDtypeStruct((n, POOL_WIDTH), F32),
        ],
        compiler_params=pltpu.CompilerParams(
            dimension_semantics=("arbitrary",), vmem_limit_bytes=VMEM_LIMIT_BYTES),
        name="proj",
    )(x2, g1, w_in, gq, gk, seg)


def _mixer_kernel(x_ref, q_ref, k_ref, v_ref, bias_ref, u_ref, up_ref, un_ref,
                  wpool_ref, pscale_ref, wout_ref, o_ref, a_scr, *, rows, seq):
    i = pl.program_id(1)
    nblk = pl.num_programs(1)
    tq = MIX_ROWS * GRID_W
    wr = min(WIN_ROWS_MAX, rows)
    nkeys = wr * GRID_W
    lane = lax.broadcasted_iota(jnp.int32, (GRID_W, LANES), 1)
    first_head = lane < NA_HEAD_DIM

    def row_body(rl, carry):
        r = i * MIX_ROWS + rl
        rs = jnp.clip(r - wr // 2, 0, rows - wr)
        cls = r - rs
        q0 = pl.multiple_of(rl * GRID_W, GRID_W)
        k0 = pl.multiple_of(rs * GRID_W, GRID_W)
        for pair in range(HEAD_PAIRS):
            sl = slice(pair * LANES, (pair + 1) * LANES)
            qs = q_ref[pl.ds(q0, GRID_W), sl]
            zero = jnp.zeros_like(qs)
            lhs = jnp.concatenate([jnp.where(first_head, qs, zero),
                                   jnp.where(first_head, zero, qs)], axis=0)
            kw = k_ref[pl.ds(k0, nkeys), sl]
            vw = v_ref[pl.ds(k0, nkeys), sl]
            s = lax.dot_general(lhs, kw, (((1,), (1,)), ((), ())),
                                preferred_element_type=F32)
            s = s + bias_ref[cls, pair]
            m = jnp.max(s, axis=-1, keepdims=True)
            p = jnp.exp(s - m)
            l = jnp.sum(p, axis=-1, keepdims=True)
            o = jnp.dot(p.astype(BF16), vw, preferred_element_type=F32) / l
            oc = jnp.where(first_head, o[:GRID_W], o[GRID_W:])
            a_scr[pl.ds(q0, GRID_W), sl] = oc.astype(BF16)
        return carry

    lax.fori_loop(0, MIX_ROWS, row_body, 0)

    prev_ok = (i > 0).astype(F32)
    next_ok = (i < nblk - 1).astype(F32)
    half = POOL_HALO // 2
    ext = jnp.concatenate([up_ref[half:, :] * prev_ok, u_ref[...],
                           un_ref[:half, :] * next_ok], axis=0)
    next_rows = ext.shape[0]
    t = i * tq + lax.broadcasted_iota(jnp.int32, (tq, 1), 0)
    pooled = []
    for g, w in enumerate(POOL_WINDOWS):
        e = ext[:, g * POOL_GROUP_DIM:(g + 1) * POOL_GROUP_DIM]
        span = 1
        while span < w:
            e = e + pltpu.roll(e, next_rows - span, axis=0)
            span *= 2
        start = half - w // 2
        if start:
            e = pltpu.roll(e, next_rows - start, axis=0)
        cnt = jnp.minimum(t + w // 2, seq) - jnp.maximum(t - w // 2, 0)
        mean = e[:tq] / cnt.astype(F32)
        pooled.append(mean - ext[half:half + tq, g * POOL_GROUP_DIM:(g + 1) * POOL_GROUP_DIM])
    d = jnp.concatenate(pooled, axis=-1).astype(BF16)
    y = jnp.dot(d, wpool_ref[...], preferred_element_type=F32) * pscale_ref[...]

    mix = jnp.concatenate([a_scr[...], y.astype(BF16)], axis=-1)
    o_ref[...] = x_ref[...] + jnp.dot(mix, wout_ref[...], preferred_element_type=F32)


def _mixer_call(x2, q, k, v, bias, u, wpool_bd, pscale, w_out, batch, seq):
    rows = seq // GRID_W
    tq = MIX_ROWS * GRID_W
    nblk = seq // tq
    halo_per_blk = tq // POOL_HALO
    n_halo = seq // POOL_HALO
    tok = lambda b, i: (b * nblk + i, 0)
    per_batch = lambda b, i: (b, 0)
    const2 = lambda b, i: (0, 0)
    prev_halo = lambda b, i: (b * n_halo + jnp.maximum(i * halo_per_blk - 1, 0), 0)
    next_halo = lambda b, i: (b * n_halo + jnp.minimum((i + 1) * halo_per_blk, n_halo - 1), 0)
    kernel = functools.partial(_mixer_kernel, rows=rows, seq=seq)
    return pl.pallas_call(
        kernel,
        grid=(batch, nblk),
        in_specs=[
            pl.BlockSpec((tq, D_MODEL), tok),
            pl.BlockSpec((tq, NA_WIDTH), tok),
            pl.BlockSpec((seq, NA_WIDTH), per_batch),
            pl.BlockSpec((seq, NA_WIDTH), per_batch),
            pl.BlockSpec(bias.shape, lambda b, i: (0, 0, 0, 0), pipeline_mode=pl.Buffered(1)),
            pl.BlockSpec((tq, POOL_WIDTH), tok),
            pl.BlockSpec((POOL_HALO, POOL_WIDTH), prev_halo),
            pl.BlockSpec((POOL_HALO, POOL_WIDTH), next_halo),
            pl.BlockSpec((POOL_WIDTH, POOL_WIDTH), const2),
            pl.BlockSpec((1, POOL_WIDTH), const2),
            pl.BlockSpec((D_MODEL, D_MODEL), const2),
        ],
        out_specs=pl.BlockSpec((tq, D_MODEL), tok),
        out_shape=jax.ShapeDtypeStruct(x2.shape, F32),
        scratch_shapes=[pltpu.VMEM((tq, NA_WIDTH), BF16)],
        compiler_params=pltpu.CompilerParams(
            dimension_semantics=("arbitrary", "arbitrary"), vmem_limit_bytes=VMEM_LIMIT_BYTES),
        name="mixer",
    )(x2, q, k, v, bias, u, u, u, wpool_bd, pscale, w_out)


def _ffn_kernel(x_ref, g2_ref, wg_ref, wu_ref, wd_ref, o_ref):
    x = x_ref[...]
    h = (x * _rms_scale(x) * g2_ref[...]).astype(BF16)
    gate = jnp.dot(h, wg_ref[...], preferred_element_type=F32)
    up = jnp.dot(h, wu_ref[...], preferred_element_type=F32)
    act = (gate * jax.nn.sigmoid(gate) * up).astype(BF16)
    o_ref[...] = x + jnp.dot(act, wd_ref[...], preferred_element_type=F32)


def _ffn_call(x2, g2, w_gate, w_up, w_down):
    n = x2.shape[0]
    tm = FFN_BLOCK
    d_ff = w_gate.shape[1]
    const = lambda i: (0, 0)
    row = lambda i: (i, 0)
    resident = pl.Buffered(1)
    return pl.pallas_call(
        _ffn_kernel,
        grid=(n // tm,),
        in_specs=[
            pl.BlockSpec((tm, D_MODEL), row),
            pl.BlockSpec((1, D_MODEL), const),
            pl.BlockSpec((D_MODEL, d_ff), const, pipeline_mode=resident),
            pl.BlockSpec((D_MODEL, d_ff), const, pipeline_mode=resident),
            pl.BlockSpec((d_ff, D_MODEL), const, pipeline_mode=resident),
        ],
        out_specs=pl.BlockSpec((tm, D_MODEL), row),
        out_shape=jax.ShapeDtypeStruct(x2.shape, F32),
        compiler_params=pltpu.CompilerParams(
            dimension_semantics=("arbitrary",), vmem_limit_bytes=VMEM_LIMIT_BYTES),
        name="ffn",
    )(x2, g2, w_gate, w_up, w_down)


def _bias_table(rpb, rows):
    wr = min(WIN_ROWS_MAX, rows)
    ncls = wr
    c = np.arange(GRID_W)
    col_start = np.clip(c - WIN_COLS // 2, 0, GRID_W - WIN_COLS)
    kc = np.arange(GRID_W)
    inside = (kc[None, :] >= col_start[:, None]) & (kc[None, :] < col_start[:, None] + WIN_COLS)
    col_rel = np.clip(kc[None, :] - c[:, None] + WIN_COLS - 1, 0, 2 * WIN_COLS - 2)
    w = np.arange(wr)
    cls = np.arange(ncls)
    row_rel = np.clip(w[None, :] - cls[:, None] + WIN_ROWS_MAX - 1, 0, 2 * WIN_ROWS_MAX - 2)
    tbl = rpb.astype(F32)[:, row_rel[:, :, None, None], col_rel[None, None, :, :]]
    tbl = jnp.where(inside[None, None, None], tbl, MASK_VALUE)
    tbl = jnp.transpose(tbl, (1, 0, 3, 2, 4))
    return tbl.reshape(ncls, HEAD_PAIRS, 2 * GRID_W, wr * GRID_W)


def _block_diag(w_pool):
    g, d, _ = w_pool.shape
    eye = jnp.eye(g, dtype=w_pool.dtype)
    return jnp.einsum('gcd,gh->gchd', w_pool, eye).reshape(g * d, g * d)


def kernel(x, norm1_g, w_in, q_norm_g, k_norm_g, rpb, w_pool, pool_scale, w_out,
           norm2_g, w_gate, w_up, w_down):
    batch, seq, _ = x.shape
    rows = seq // GRID_W
    depth = w_in.shape[0]
    seg = jnp.asarray(np.kron(np.eye(NA_HEADS), np.full((NA_HEAD_DIM, NA_HEAD_DIM), 1.0 / NA_HEAD_DIM)),
                      dtype=BF16)
    x2 = x.reshape(batch * seq, D_MODEL)
    for l in range(depth):
        gq = jnp.tile(q_norm_g[l], NA_HEADS)[None, :]
        gk = jnp.tile(k_norm_g[l], NA_HEADS)[None, :]
        q, k, v, u = _proj_call(x2, norm1_g[l][None, :], w_in[l].astype(BF16), gq, gk, seg)
        x2 = _mixer_call(x2, q, k, v, _bias_table(rpb[l], rows), u,
                         _block_diag(w_pool[l]).astype(BF16), pool_scale[l][None, :],
                         w_out[l].astype(BF16), batch, seq)
        x2 = _ffn_call(x2, norm2_g[l][None, :], w_gate[l].astype(BF16),
                       w_up[l].astype(BF16), w_down[l].astype(BF16))
    return x2.reshape(batch, seq, D_MODEL)
```

```python
import functools

import jax
import jax.numpy as jnp
import numpy as np
from jax import lax
from jax.experimental import pallas as pl
from jax.experimental.pallas import tpu as pltpu

D_MODEL = 1024
GRID_W = 64
NA_WIDTH = D_MODEL // 2
NA_HEADS = 8
NA_HEAD_DIM = NA_WIDTH // NA_HEADS
WIN_ROWS_MAX = 8
WIN_COLS = 16
POOL_WIDTH = D_MODEL - NA_WIDTH
POOL_WINDOWS = (2, 4, 8, 16)
POOL_GROUPS = len(POOL_WINDOWS)
POOL_GROUP_DIM = POOL_WIDTH // POOL_GROUPS
IN_WIDTH = 3 * NA_WIDTH + POOL_WIDTH
EPS = 1e-6

LANES = 128
HEAD_PAIRS = NA_WIDTH // LANES
MASK_VALUE = -1e30
POOL_HALO = 16
VMEM_LIMIT_BYTES = 56 * 1024 * 1024

PROJ_BLOCK = 512
MIX_ROWS = 8
FFN_BLOCK = 512

F32 = jnp.float32
BF16 = jnp.bfloat16


def _rms_scale(x):
    return lax.rsqrt(jnp.mean(x * x, axis=-1, keepdims=True) + EPS)


def _proj_kernel(x_ref, g1_ref, w_ref, gq_ref, gk_ref, seg_ref,
                 q_ref, k_ref, v_ref, u_ref):
    x = x_ref[...]
    h = (x * _rms_scale(x) * g1_ref[...]).astype(BF16)
    proj = jnp.dot(h, w_ref[...], preferred_element_type=F32)

    def head_norm(t, g):
        ms = jnp.dot((t * t).astype(BF16), seg_ref[...], preferred_element_type=F32)
        return t * lax.rsqrt(ms + EPS) * g

    q = head_norm(proj[:, :NA_WIDTH], gq_ref[...])
    k = head_norm(proj[:, NA_WIDTH:2 * NA_WIDTH], gk_ref[...])
    q_ref[...] = (q * (NA_HEAD_DIM ** -0.5)).astype(BF16)
    k_ref[...] = k.astype(BF16)
    v_ref[...] = proj[:, 2 * NA_WIDTH:3 * NA_WIDTH].astype(BF16)
    u_ref[...] = proj[:, 3 * NA_WIDTH:]


def _proj_call(x2, g1, w_in, gq, gk, seg):
    n = x2.shape[0]
    tm = PROJ_BLOCK
    const = lambda i: (0, 0)
    row = lambda i: (i, 0)
    return pl.pallas_call(
        _proj_kernel,
        grid=(n // tm,),
        in_specs=[
            pl.BlockSpec((tm, D_MODEL), row),
            pl.BlockSpec((1, D_MODEL), const),
            pl.BlockSpec((D_MODEL, IN_WIDTH), const),
            pl.BlockSpec((1, NA_WIDTH), const),
            pl.BlockSpec((1, NA_WIDTH), const),
            pl.BlockSpec((NA_WIDTH, NA_WIDTH), const),
        ],
        out_specs=[
            pl.BlockSpec((tm, NA_WIDTH), row),
            pl.BlockSpec((tm, NA_WIDTH), row),
            pl.BlockSpec((tm, NA_WIDTH), row),
            pl.BlockSpec((tm, POOL_WIDTH), row),
        ],
        out_shape=[
            jax.ShapeDtypeStruct((n, NA_WIDTH), BF16),
            jax.ShapeDtypeStruct((n, NA_WIDTH), BF16),
            jax.ShapeDtypeStruct((n, NA_WIDTH), BF16),
            jax.ShapeDtypeStruct((n, POOL_WIDTH), F32),
        ],
        compiler_params=pltpu.CompilerParams(
            dimension_semantics=("arbitrary",), vmem_limit_bytes=VMEM_LIMIT_BYTES),
        name="proj",
    )(x2, g1, w_in, gq, gk, seg)


def _mixer_kernel(x_ref, q_ref, k_ref, v_ref, bias_ref, u_ref, up_ref, un_ref,
                  wpool_ref, pscale_ref, wout_ref, o_ref, a_scr, *, rows, seq):
    i = pl.program_id(1)
    nblk = pl.num_programs(1)
    tq = MIX_ROWS * GRID_W
    wr = min(WIN_ROWS_MAX, rows)
    nkeys = wr * GRID_W
    lane = lax.broadcasted_iota(jnp.int32, (GRID_W, LANES), 1)
    first_head = lane < NA_HEAD_DIM

    def row_body(rl, carry):
        r = i * MIX_ROWS + rl
        rs = jnp.clip(r - wr // 2, 0, rows - wr)
        cls = r - rs
        q0 = pl.multiple_of(rl * GRID_W, GRID_W)
        k0 = pl.multiple_of(rs * GRID_W, GRID_W)
        for pair in range(HEAD_PAIRS):
            sl = slice(pair * LANES, (pair + 1) * LANES)
            qs = q_ref[pl.ds(q0, GRID_W), sl]
            zero = jnp.zeros_like(qs)
            lhs = jnp.concatenate([jnp.where(first_head, qs, zero),
                                   jnp.where(first_head, zero, qs)], axis=0)
            kw = k_ref[pl.ds(k0, nkeys), sl]
            vw = v_ref[pl.ds(k0, nkeys), sl]
            s = lax.dot_general(lhs, kw, (((1,), (1,)), ((), ())),
                                preferred_element_type=F32)
            s = s + bias_ref[cls, pair]
            m = jnp.max(s, axis=-1, keepdims=True)
            p = jnp.exp(s - m)
            l = jnp.sum(p, axis=-1, keepdims=True)
            o = jnp.dot(p.astype(BF16), vw, preferred_element_type=F32) / l
            oc = jnp.where(first_head, o[:GRID_W], o[GRID_W:])
            a_scr[pl.ds(q0, GRID_W), sl] = oc.astype(BF16)
        return carry

    lax.fori_loop(0, MIX_ROWS, row_body, 0)

    prev_ok = (i > 0).astype(F32)
    next_ok = (i < nblk - 1).astype(F32)
    half = POOL_HALO // 2
    ext = jnp.concatenate([up_ref[half:, :] * prev_ok, u_ref[...],
                           un_ref[:half, :] * next_ok], axis=0)
    next_rows = ext.shape[0]
    t = i * tq + lax.broadcasted_iota(jnp.int32, (tq, 1), 0)
    pooled = []
    for g, w in enumerate(POOL_WINDOWS):
        e = ext[:, g * POOL_GROUP_DIM:(g + 1) * POOL_GROUP_DIM]
        span = 1
        while span < w:
            e = e + pltpu.roll(e, next_rows - span, axis=0)
            span *= 2
        start = half - w // 2
        if start:
            e = pltpu.roll(e, next_rows - start, axis=0)
        cnt = jnp.minimum(t + w // 2, seq) - jnp.maximum(t - w // 2, 0)
        mean = e[:tq] / cnt.astype(F32)
        pooled.append(mean - ext[half:half + tq, g * POOL_GROUP_DIM:(g + 1) * POOL_GROUP_DIM])
    d = jnp.concatenate(pooled, axis=-1).astype(BF16)
    y = jnp.dot(d, wpool_ref[...], preferred_element_type=F32) * pscale_ref[...]

    mix = jnp.concatenate([a_scr[...], y.astype(BF16)], axis=-1)
    o_ref[...] = x_ref[...] + jnp.dot(mix, wout_ref[...], preferred_element_type=F32)


def _mixer_call(x2, q, k, v, bias, u, wpool_bd, pscale, w_out, batch, seq):
    rows = seq // GRID_W
    tq = MIX_ROWS * GRID_W
    nblk = seq // tq
    halo_per_blk = tq // POOL_HALO
    n_halo = seq // POOL_HALO
    tok = lambda b, i: (b * nblk + i, 0)
    per_batch = lambda b, i: (b, 0)
    const2 = lambda b, i: (0, 0)
    prev_halo = lambda b, i: (b * n_halo + jnp.maximum(i * halo_per_blk - 1, 0), 0)
    next_halo = lambda b, i: (b * n_halo + jnp.minimum((i + 1) * halo_per_blk, n_halo - 1), 0)
    kernel = functools.partial(_mixer_kernel, rows=rows, seq=seq)
    return pl.pallas_call(
        kernel,
        grid=(batch, nblk),
        in_specs=[
            pl.BlockSpec((tq, D_MODEL), tok),
            pl.BlockSpec((tq, NA_WIDTH), tok),
            pl.BlockSpec((seq, NA_WIDTH), per_batch),
            pl.BlockSpec((seq, NA_WIDTH), per_batch),
            pl.BlockSpec(bias.shape, lambda b, i: (0, 0, 0, 0), pipeline_mode=pl.Buffered(1)),
            pl.BlockSpec((tq, POOL_WIDTH), tok),
            pl.BlockSpec((POOL_HALO, POOL_WIDTH), prev_halo),
            pl.BlockSpec((POOL_HALO, POOL_WIDTH), next_halo),
            pl.BlockSpec((POOL_WIDTH, POOL_WIDTH), const2),
            pl.BlockSpec((1, POOL_WIDTH), const2),
            pl.BlockSpec((D_MODEL, D_MODEL), const2),
        ],
        out_specs=pl.BlockSpec((tq, D_MODEL), tok),
        out_shape=jax.ShapeDtypeStruct(x2.shape, F32),
        scratch_shapes=[pltpu.VMEM((tq, NA_WIDTH), BF16)],
        compiler_params=pltpu.CompilerParams(
            dimension_semantics=("arbitrary", "arbitrary"), vmem_limit_bytes=VMEM_LIMIT_BYTES),
        name="mixer",
    )(x2, q, k, v, bias, u, u, u, wpool_bd, pscale, w_out)


def _ffn_kernel(x_ref, g2_ref, wg_ref, wu_ref, wd_ref, o_ref):
    x = x_ref[...]
    h = (x * _rms_scale(x) * g2_ref[...]).astype(BF16)
    gate = jnp.dot(h, wg_ref[...], preferred_element_type=F32)
    up = jnp.dot(h, wu_ref[...], preferred_element_type=F32)
    act = (gate * jax.nn.sigmoid(gate) * up).astype(BF16)
    o_ref[...] = x + jnp.dot(act, wd_ref[...], preferred_element_type=F32)


def _ffn_call(x2, g2, w_gate, w_up, w_down):
    n = x2.shape[0]
    tm = FFN_BLOCK
    d_ff = w_gate.shape[1]
    const = lambda i: (0, 0)
    row = lambda i: (i, 0)
    resident = pl.Buffered(1)
    return pl.pallas_call(
        _ffn_kernel,
        grid=(n // tm,),
        in_specs=[
            pl.BlockSpec((tm, D_MODEL), row),
            pl.BlockSpec((1, D_MODEL), const),
            pl.BlockSpec((D_MODEL, d_ff), const, pipeline_mode=resident),
            pl.BlockSpec((D_MODEL, d_ff), const, pipeline_mode=resident),
            pl.BlockSpec((d_ff, D_MODEL), const, pipeline_mode=resident),
        ],
        out_specs=pl.BlockSpec((tm, D_MODEL), row),
        out_shape=jax.ShapeDtypeStruct(x2.shape, F32),
        compiler_params=pltpu.CompilerParams(
            dimension_semantics=("arbitrary",), vmem_limit_bytes=VMEM_LIMIT_BYTES),
        name="ffn",
    )(x2, g2, w_gate, w_up, w_down)


def _bias_table(rpb, rows):
    wr = min(WIN_ROWS_MAX, rows)
    ncls = wr
    c = np.arange(GRID_W)
    col_start = np.clip(c - WIN_COLS // 2, 0, GRID_W - WIN_COLS)
    kc = np.arange(GRID_W)
    inside = (kc[None, :] >= col_start[:, None]) & (kc[None, :] < col_start[:, None] + WIN_COLS)
    col_rel = kc[None, :] - c[:, None] + WIN_COLS - 1
    onehot = (col_rel[:, :, None] == np.arange(2 * WIN_COLS - 1)) & inside[:, :, None]
    cols = jnp.einsum('hrj,ckj->hrck', rpb.astype(F32), jnp.asarray(onehot, F32),
                      precision=lax.Precision.HIGHEST)
    cols = jnp.where(inside[None, None], cols, MASK_VALUE)
    per_cls = [cols[:, WIN_ROWS_MAX - 1 - s:WIN_ROWS_MAX - 1 - s + wr] for s in range(ncls)]
    tbl = jnp.stack(per_cls, axis=0)
    tbl = jnp.transpose(tbl, (0, 1, 3, 2, 4))
    return tbl.reshape(ncls, HEAD_PAIRS, 2 * GRID_W, wr * GRID_W)


def _block_diag(w_pool):
    g, d, _ = w_pool.shape
    eye = jnp.eye(g, dtype=w_pool.dtype)
    return jnp.einsum('gcd,gh->gchd', w_pool, eye).reshape(g * d, g * d)


def kernel(x, norm1_g, w_in, q_norm_g, k_norm_g, rpb, w_pool, pool_scale, w_out,
           norm2_g, w_gate, w_up, w_down):
    batch, seq, _ = x.shape
    rows = seq // GRID_W
    depth = w_in.shape[0]
    seg = jnp.asarray(np.kron(np.eye(NA_HEADS), np.full((NA_HEAD_DIM, NA_HEAD_DIM), 1.0 / NA_HEAD_DIM)),
                      dtype=BF16)
    x2 = x.reshape(batch * seq, D_MODEL)
    for l in range(depth):
        gq = jnp.tile(q_norm_g[l], NA_HEADS)[None, :]
        gk = jnp.tile(k_norm_g[l], NA_HEADS)[None, :]
        q, k, v, u = _proj_call(x2, norm1_g[l][None, :], w_in[l].astype(BF16), gq, gk, seg)
        x2 = _mixer_call(x2, q, k, v, _bias_table(rpb[l], rows), u,
                         _block_diag(w_pool[l]).astype(BF16), pool_scale[l][None, :],
                         w_out[l].astype(BF16), batch, seq)
        x2 = _ffn_call(x2, norm2_g[l][None, :], w_gate[l].astype(BF16),
                       w_up[l].astype(BF16), w_down[l].astype(BF16))
    return x2.reshape(batch, seq, D_MODEL)
```

```python
import functools

import jax
import jax.numpy as jnp
import numpy as np
from jax import lax
from jax.experimental import pallas as pl
from jax.experimental.pallas import tpu as pltpu

D_MODEL = 1024
GRID_W = 64
NA_WIDTH = D_MODEL // 2
NA_HEADS = 8
NA_HEAD_DIM = NA_WIDTH // NA_HEADS
WIN_ROWS_MAX = 8
WIN_COLS = 16
POOL_WIDTH = D_MODEL - NA_WIDTH
POOL_WINDOWS = (2, 4, 8, 16)
POOL_GROUPS = len(POOL_WINDOWS)
POOL_GROUP_DIM = POOL_WIDTH // POOL_GROUPS
IN_WIDTH = 3 * NA_WIDTH + POOL_WIDTH
EPS = 1e-6

LANES = 128
HEAD_PAIRS = NA_WIDTH // LANES
MASK_VALUE = -1e30
POOL_HALO = 16
VMEM_LIMIT_BYTES = 56 * 1024 * 1024

PROJ_BLOCK = 512
MIX_BLOCK = 512
FFN_BLOCK = 512

F32 = jnp.float32
BF16 = jnp.bfloat16


def _rms_scale(x):
    return lax.rsqrt(jnp.mean(x * x, axis=-1, keepdims=True) + EPS)


def _proj_kernel(x_ref, g1_ref, w_ref, gq_ref, gk_ref, seg_ref,
                 q_ref, k_ref, v_ref, u_ref):
    x = x_ref[...]
    h = (x * _rms_scale(x) * g1_ref[...]).astype(BF16)
    proj = jnp.dot(h, w_ref[...], preferred_element_type=F32)

    def head_norm(t, g):
        ms = jnp.dot((t * t).astype(BF16), seg_ref[...], preferred_element_type=F32)
        return t * lax.rsqrt(ms + EPS) * g

    q = head_norm(proj[:, :NA_WIDTH], gq_ref[...])
    k = head_norm(proj[:, NA_WIDTH:2 * NA_WIDTH], gk_ref[...])
    q_ref[...] = (q * (NA_HEAD_DIM ** -0.5)).astype(BF16)
    k_ref[...] = k.astype(BF16)
    v_ref[...] = proj[:, 2 * NA_WIDTH:3 * NA_WIDTH].astype(BF16)
    u_ref[...] = proj[:, 3 * NA_WIDTH:]


def _proj_call(x2, g1, w_in, gq, gk, seg):
    n = x2.shape[0]
    tm = PROJ_BLOCK
    const = lambda i: (0, 0)
    row = lambda i: (i, 0)
    return pl.pallas_call(
        _proj_kernel,
        grid=(n // tm,),
        in_specs=[
            pl.BlockSpec((tm, D_MODEL), row),
            pl.BlockSpec((1, D_MODEL), const),
            pl.BlockSpec((D_MODEL, IN_WIDTH), const),
            pl.BlockSpec((1, NA_WIDTH), const),
            pl.BlockSpec((1, NA_WIDTH), const),
            pl.BlockSpec((NA_WIDTH, NA_WIDTH), const),
        ],
        out_specs=[
            pl.BlockSpec((tm, NA_WIDTH), row),
            pl.BlockSpec((tm, NA_WIDTH), row),
            pl.BlockSpec((tm, NA_WIDTH), row),
            pl.BlockSpec((tm, POOL_WIDTH), row),
        ],
        out_shape=[
            jax.ShapeDtypeStruct((n, NA_WIDTH), BF16),
            jax.ShapeDtypeStruct((n, NA_WIDTH), BF16),
            jax.ShapeDtypeStruct((n, NA_WIDTH), BF16),
            jax.ShapeDtypeStruct((n, POOL_WIDTH), F32),
        ],
        compiler_params=pltpu.CompilerParams(
            dimension_semantics=("arbitrary",), vmem_limit_bytes=VMEM_LIMIT_BYTES),
        name="proj",
    )(x2, g1, w_in, gq, gk, seg)


def _attn_kernel(q_ref, k_ref, v_ref, bias_ref, a_ref, s_scr, p_scr, l_scr, *, rows):
    wr = min(WIN_ROWS_MAX, rows)
    nkeys = wr * GRID_W
    lane = lax.broadcasted_iota(jnp.int32, (GRID_W, LANES), 1)
    first_head = lane < NA_HEAD_DIM
    pair_lanes = [slice(p * LANES, (p + 1) * LANES) for p in range(HEAD_PAIRS)]

    def window_start(r):
        return jnp.clip(r - wr // 2, 0, rows - wr)

    def scores(r, slot):
        rs = window_start(r)
        cls = r - rs
        q0 = pl.multiple_of(r * GRID_W, GRID_W)
        k0 = pl.multiple_of(rs * GRID_W, GRID_W)
        for pair, sl in enumerate(pair_lanes):
            qs = q_ref[pl.ds(q0, GRID_W), sl]
            zero = jnp.zeros_like(qs)
            lhs = jnp.concatenate([jnp.where(first_head, qs, zero),
                                   jnp.where(first_head, zero, qs)], axis=0)
            kw = k_ref[pl.ds(k0, nkeys), sl]
            s = lax.dot_general(lhs, kw, (((1,), (1,)), ((), ())),
                                preferred_element_type=F32)
            s_scr[slot, pair] = s + bias_ref[cls, pair]

    def softmax(slot):
        for pair in range(HEAD_PAIRS):
            s = s_scr[slot, pair]
            m = jnp.max(s, axis=-1, keepdims=True)
            p = jnp.exp(s - m)
            l = jnp.sum(p, axis=-1, keepdims=True)
            l_scr[slot, pair] = jnp.broadcast_to(l, (2 * GRID_W, LANES))
            p_scr[slot, pair] = p.astype(BF16)

    def weighted_values(r, slot):
        q0 = pl.multiple_of(r * GRID_W, GRID_W)
        k0 = pl.multiple_of(window_start(r) * GRID_W, GRID_W)
        for pair, sl in enumerate(pair_lanes):
            vw = v_ref[pl.ds(k0, nkeys), sl]
            o = jnp.dot(p_scr[slot, pair], vw, preferred_element_type=F32) / l_scr[slot, pair]
            oc = jnp.where(first_head, o[:GRID_W], o[GRID_W:])
            a_ref[pl.ds(q0, GRID_W), sl] = oc.astype(BF16)

    scores(0, 0)
    scores(1, 1)
    softmax(0)

    def two_rows(j, carry):
        r = 2 * j
        scores(jnp.minimum(r + 2, rows - 1), 0)
        softmax(1)
        weighted_values(r, 0)
        scores(jnp.minimum(r + 3, rows - 1), 1)
        softmax(0)
        weighted_values(r + 1, 1)
        return carry

    lax.fori_loop(0, rows // 2, two_rows, 0)


def _attn_call(q, k, v, bias, batch, seq):
    rows = seq // GRID_W
    nkeys = min(WIN_ROWS_MAX, rows) * GRID_W
    per_batch = lambda b: (b, 0)
    blk = pl.BlockSpec((seq, NA_WIDTH), per_batch)
    kernel = functools.partial(_attn_kernel, rows=rows)
    return pl.pallas_call(
        kernel,
        grid=(batch,),
        in_specs=[blk, blk, blk,
                  pl.BlockSpec(bias.shape, lambda b: (0, 0, 0, 0), pipeline_mode=pl.Buffered(1))],
        out_specs=blk,
        out_shape=jax.ShapeDtypeStruct(q.shape, BF16),
        scratch_shapes=[
            pltpu.VMEM((2, HEAD_PAIRS, 2 * GRID_W, nkeys), F32),
            pltpu.VMEM((2, HEAD_PAIRS, 2 * GRID_W, nkeys), BF16),
            pltpu.VMEM((2, HEAD_PAIRS, 2 * GRID_W, LANES), F32),
        ],
        compiler_params=pltpu.CompilerParams(
            dimension_semantics=("arbitrary",), vmem_limit_bytes=VMEM_LIMIT_BYTES),
        name="attn",
    )(q, k, v, bias)


def _mixer_kernel(x_ref, a_ref, u_ref, up_ref, un_ref, wpool_ref, pscale_ref, wout_ref, o_ref,
                  *, seq, blocks_per_seq):
    i = pl.program_id(0) % blocks_per_seq
    tq = x_ref.shape[0]

    prev_ok = (i > 0).astype(F32)
    next_ok = (i < blocks_per_seq - 1).astype(F32)
    half = POOL_HALO // 2
    ext = jnp.concatenate([up_ref[half:, :] * prev_ok, u_ref[...],
                           un_ref[:half, :] * next_ok], axis=0)
    next_rows = ext.shape[0]
    t = i * tq + lax.broadcasted_iota(jnp.int32, (tq, 1), 0)
    pooled = []
    for g, w in enumerate(POOL_WINDOWS):
        e = ext[:, g * POOL_GROUP_DIM:(g + 1) * POOL_GROUP_DIM]
        span = 1
        while span < w:
            e = e + pltpu.roll(e, next_rows - span, axis=0)
            span *= 2
        start = half - w // 2
        if start:
            e = pltpu.roll(e, next_rows - start, axis=0)
        cnt = jnp.minimum(t + w // 2, seq) - jnp.maximum(t - w // 2, 0)
        mean = e[:tq] / cnt.astype(F32)
        pooled.append(mean - ext[half:half + tq, g * POOL_GROUP_DIM:(g + 1) * POOL_GROUP_DIM])
    d = jnp.concatenate(pooled, axis=-1).astype(BF16)
    y = jnp.dot(d, wpool_ref[...], preferred_element_type=F32) * pscale_ref[...]

    mix = jnp.concatenate([a_ref[...], y.astype(BF16)], axis=-1)
    o_ref[...] = x_ref[...] + jnp.dot(mix, wout_ref[...], preferred_element_type=F32)


def _mixer_call(x2, a, u, wpool_bd, pscale, w_out, seq):
    n = x2.shape[0]
    tq = MIX_BLOCK
    nblk = seq // tq
    halo_per_blk = tq // POOL_HALO
    n_halo = seq // POOL_HALO
    tok = lambda g: (g, 0)
    const2 = lambda g: (0, 0)
    prev_halo = lambda g: ((g // nblk) * n_halo + jnp.maximum((g % nblk) * halo_per_blk - 1, 0), 0)
    next_halo = lambda g: ((g // nblk) * n_halo
                           + jnp.minimum((g % nblk + 1) * halo_per_blk, n_halo - 1), 0)
    kernel = functools.partial(_mixer_kernel, seq=seq, blocks_per_seq=nblk)
    return pl.pallas_call(
        kernel,
        grid=(n // tq,),
        in_specs=[
            pl.BlockSpec((tq, D_MODEL), tok),
            pl.BlockSpec((tq, NA_WIDTH), tok),
            pl.BlockSpec((tq, POOL_WIDTH), tok),
            pl.BlockSpec((POOL_HALO, POOL_WIDTH), prev_halo),
            pl.BlockSpec((POOL_HALO, POOL_WIDTH), next_halo),
            pl.BlockSpec((POOL_WIDTH, POOL_WIDTH), const2),
            pl.BlockSpec((1, POOL_WIDTH), const2),
            pl.BlockSpec((D_MODEL, D_MODEL), const2),
        ],
        out_specs=pl.BlockSpec((tq, D_MODEL), tok),
        out_shape=jax.ShapeDtypeStruct(x2.shape, F32),
        compiler_params=pltpu.CompilerParams(
            dimension_semantics=("arbitrary",), vmem_limit_bytes=VMEM_LIMIT_BYTES),
        name="mixer",
    )(x2, a, u, u, u, wpool_bd, pscale, w_out)


def _ffn_kernel(x_ref, g2_ref, wg_ref, wu_ref, wd_ref, o_ref):
    x = x_ref[...]
    h = (x * _rms_scale(x) * g2_ref[...]).astype(BF16)
    gate = jnp.dot(h, wg_ref[...], preferred_element_type=F32)
    up = jnp.dot(h, wu_ref[...], preferred_element_type=F32)
    act = (gate * jax.nn.sigmoid(gate) * up).astype(BF16)
    o_ref[...] = x + jnp.dot(act, wd_ref[...], preferred_element_type=F32)


def _ffn_call(x2, g2, w_gate, w_up, w_down):
    n = x2.shape[0]
    tm = FFN_BLOCK
    d_ff = w_gate.shape[1]
    const = lambda i: (0, 0)
    row = lambda i: (i, 0)
    resident = pl.Buffered(1)
    return pl.pallas_call(
        _ffn_kernel,
        grid=(n // tm,),
        in_specs=[
            pl.BlockSpec((tm, D_MODEL), row),
            pl.BlockSpec((1, D_MODEL), const),
            pl.BlockSpec((D_MODEL, d_ff), const, pipeline_mode=resident),
            pl.BlockSpec((D_MODEL, d_ff), const, pipeline_mode=resident),
            pl.BlockSpec((d_ff, D_MODEL), const, pipeline_mode=resident),
        ],
        out_specs=pl.BlockSpec((tm, D_MODEL), row),
        out_shape=jax.ShapeDtypeStruct(x2.shape, F32),
        compiler_params=pltpu.CompilerParams(
            dimension_semantics=("arbitrary",), vmem_limit_bytes=VMEM_LIMIT_BYTES),
        name="ffn",
    )(x2, g2, w_gate, w_up, w_down)


def _bias_table(rpb, rows):
    wr = min(WIN_ROWS_MAX, rows)
    ncls = wr
    c = np.arange(GRID_W)
    col_start = np.clip(c - WIN_COLS // 2, 0, GRID_W - WIN_COLS)
    kc = np.arange(GRID_W)
    inside = (kc[None, :] >= col_start[:, None]) & (kc[None, :] < col_start[:, None] + WIN_COLS)
    col_rel = kc[None, :] - c[:, None] + WIN_COLS - 1
    onehot = (col_rel[:, :, None] == np.arange(2 * WIN_COLS - 1)) & inside[:, :, None]
    cols = jnp.einsum('hrj,ckj->hrck', rpb.astype(F32), jnp.asarray(onehot, F32),
                      precision=lax.Precision.HIGHEST)
    cols = jnp.where(inside[None, None], cols, MASK_VALUE)
    per_cls = [cols[:, WIN_ROWS_MAX - 1 - s:WIN_ROWS_MAX - 1 - s + wr] for s in range(ncls)]
    tbl = jnp.stack(per_cls, axis=0)
    tbl = jnp.transpose(tbl, (0, 1, 3, 2, 4))
    return tbl.reshape(ncls, HEAD_PAIRS, 2 * GRID_W, wr * GRID_W)


def _block_diag(w_pool):
    g, d, _ = w_pool.shape
    eye = jnp.eye(g, dtype=w_pool.dtype)
    return jnp.einsum('gcd,gh->gchd', w_pool, eye).reshape(g * d, g * d)


def kernel(x, norm1_g, w_in, q_norm_g, k_norm_g, rpb, w_pool, pool_scale, w_out,
           norm2_g, w_gate, w_up, w_down):
    batch, seq, _ = x.shape
    rows = seq // GRID_W
    depth = w_in.shape[0]
    seg = jnp.asarray(np.kron(np.eye(NA_HEADS), np.full((NA_HEAD_DIM, NA_HEAD_DIM), 1.0 / NA_HEAD_DIM)),
                      dtype=BF16)
    x2 = x.reshape(batch * seq, D_MODEL)
    for l in range(depth):
        gq = jnp.tile(q_norm_g[l], NA_HEADS)[None, :]
        gk = jnp.tile(k_norm_g[l], NA_HEADS)[None, :]
        q, k, v, u = _proj_call(x2, norm1_g[l][None, :], w_in[l].astype(BF16), gq, gk, seg)
        a = _attn_call(q, k, v, _bias_table(rpb[l], rows), batch, seq)
        x2 = _mixer_call(x2, a, u, _block_diag(w_pool[l]).astype(BF16), pool_scale[l][None, :],
                         w_out[l].astype(BF16), seq)
        x2 = _ffn_call(x2, norm2_g[l][None, :], w_gate[l].astype(BF16),
                       w_up[l].astype(BF16), w_down[l].astype(BF16))
    return x2.reshape(batch, seq, D_MODEL)
```

```python
import functools

import jax
import jax.numpy as jnp
import numpy as np
from jax import lax
from jax.experimental import pallas as pl
from jax.experimental.pallas import tpu as pltpu

D_MODEL = 1024
GRID_W = 64
NA_WIDTH = D_MODEL // 2
NA_HEADS = 8
NA_HEAD_DIM = NA_WIDTH // NA_HEADS
WIN_ROWS_MAX = 8
WIN_COLS = 16
POOL_WIDTH = D_MODEL - NA_WIDTH
POOL_WINDOWS = (2, 4, 8, 16)
POOL_GROUPS = len(POOL_WINDOWS)
POOL_GROUP_DIM = POOL_WIDTH // POOL_GROUPS
IN_WIDTH = 3 * NA_WIDTH + POOL_WIDTH
EPS = 1e-6

LANES = 128
HEAD_PAIRS = NA_WIDTH // LANES
MASK_VALUE = -1e30
POOL_HALO = 16
VMEM_LIMIT_BYTES = 56 * 1024 * 1024

PROJ_BLOCK = 512
ATTN_ROWS_PER_STEP = 4
TAIL_BLOCK = 512

F32 = jnp.float32
BF16 = jnp.bfloat16


def _rms_scale(x):
    return lax.rsqrt(jnp.mean(x * x, axis=-1, keepdims=True) + EPS)


def _proj_kernel(x_ref, g1_ref, w_ref, gq_ref, gk_ref, seg_ref,
                 q_ref, k_ref, v_ref, u_ref):
    x = x_ref[...]
    h = (x * _rms_scale(x) * g1_ref[...]).astype(BF16)
    proj = jnp.dot(h, w_ref[...], preferred_element_type=F32)

    def head_norm(t, g):
        ms = jnp.dot((t * t).astype(BF16), seg_ref[...], preferred_element_type=F32)
        return t * lax.rsqrt(ms + EPS) * g

    q = head_norm(proj[:, :NA_WIDTH], gq_ref[...])
    k = head_norm(proj[:, NA_WIDTH:2 * NA_WIDTH], gk_ref[...])
    q_ref[...] = (q * (NA_HEAD_DIM ** -0.5)).astype(BF16)
    k_ref[...] = k.astype(BF16)
    v_ref[...] = proj[:, 2 * NA_WIDTH:3 * NA_WIDTH].astype(BF16)
    u_ref[...] = proj[:, 3 * NA_WIDTH:]


def _proj_call(x2, g1, w_in, gq, gk, seg):
    n = x2.shape[0]
    tm = PROJ_BLOCK
    const = lambda i: (0, 0)
    row = lambda i: (i, 0)
    return pl.pallas_call(
        _proj_kernel,
        grid=(n // tm,),
        in_specs=[
            pl.BlockSpec((tm, D_MODEL), row),
            pl.BlockSpec((1, D_MODEL), const),
            pl.BlockSpec((D_MODEL, IN_WIDTH), const),
            pl.BlockSpec((1, NA_WIDTH), const),
            pl.BlockSpec((1, NA_WIDTH), const),
            pl.BlockSpec((NA_WIDTH, NA_WIDTH), const),
        ],
        out_specs=[
            pl.BlockSpec((tm, NA_WIDTH), row),
            pl.BlockSpec((tm, NA_WIDTH), row),
            pl.BlockSpec((tm, NA_WIDTH), row),
            pl.BlockSpec((tm, POOL_WIDTH), row),
        ],
        out_shape=[
            jax.ShapeDtypeStruct((n, NA_WIDTH), BF16),
            jax.ShapeDtypeStruct((n, NA_WIDTH), BF16),
            jax.ShapeDtypeStruct((n, NA_WIDTH), BF16),
            jax.ShapeDtypeStruct((n, POOL_WIDTH), F32),
        ],
        compiler_params=pltpu.CompilerParams(
            dimension_semantics=("arbitrary",), vmem_limit_bytes=VMEM_LIMIT_BYTES),
        name="proj",
    )(x2, g1, w_in, gq, gk, seg)


def _attn_kernel(q_ref, k_ref, v_ref, bias_ref, a_ref, s_scr, p_scr, l_scr, *, rows):
    wr = min(WIN_ROWS_MAX, rows)
    nkeys = wr * GRID_W
    lane = lax.broadcasted_iota(jnp.int32, (GRID_W, LANES), 1)
    first_head = lane < NA_HEAD_DIM
    pair_lanes = [slice(p * LANES, (p + 1) * LANES) for p in range(HEAD_PAIRS)]

    def window_start(r):
        return jnp.clip(r - wr // 2, 0, rows - wr)

    def scores(r, slot):
        rs = window_start(r)
        cls = r - rs
        q0 = pl.multiple_of(r * GRID_W, GRID_W)
        k0 = pl.multiple_of(rs * GRID_W, GRID_W)
        for pair, sl in enumerate(pair_lanes):
            qs = q_ref[pl.ds(q0, GRID_W), sl]
            zero = jnp.zeros_like(qs)
            lhs = jnp.concatenate([jnp.where(first_head, qs, zero),
                                   jnp.where(first_head, zero, qs)], axis=0)
            kw = k_ref[pl.ds(k0, nkeys), sl]
            s = lax.dot_general(lhs, kw, (((1,), (1,)), ((), ())),
                                preferred_element_type=F32)
            s_scr[slot, pair] = s + bias_ref[cls, pair]

    def softmax(slot):
        for pair in range(HEAD_PAIRS):
            s = s_scr[slot, pair]
            m = jnp.max(s, axis=-1, keepdims=True)
            p = jnp.exp(s - m)
            l = jnp.sum(p, axis=-1, keepdims=True)
            l_scr[slot, pair] = jnp.broadcast_to(l, (2 * GRID_W, LANES))
            p_scr[slot, pair] = p.astype(BF16)

    def weighted_values(r, slot):
        q0 = pl.multiple_of(r * GRID_W, GRID_W)
        k0 = pl.multiple_of(window_start(r) * GRID_W, GRID_W)
        for pair, sl in enumerate(pair_lanes):
            vw = v_ref[pl.ds(k0, nkeys), sl]
            o = jnp.dot(p_scr[slot, pair], vw, preferred_element_type=F32) / l_scr[slot, pair]
            oc = jnp.where(first_head, o[:GRID_W], o[GRID_W:])
            a_ref[pl.ds(q0, GRID_W), sl] = oc.astype(BF16)

    scores(0, 0)
    scores(1, 1)
    softmax(0)

    def row_group(j, carry):
        for k in range(ATTN_ROWS_PER_STEP):
            r = ATTN_ROWS_PER_STEP * j + k
            scores(jnp.minimum(r + 2, rows - 1), k % 2)
            softmax((k + 1) % 2)
            weighted_values(r, k % 2)
        return carry

    lax.fori_loop(0, rows // ATTN_ROWS_PER_STEP, row_group, 0)


def _attn_call(q, k, v, bias, batch, seq):
    rows = seq // GRID_W
    nkeys = min(WIN_ROWS_MAX, rows) * GRID_W
    per_batch = lambda b: (b, 0)
    blk = pl.BlockSpec((seq, NA_WIDTH), per_batch)
    kernel = functools.partial(_attn_kernel, rows=rows)
    return pl.pallas_call(
        kernel,
        grid=(batch,),
        in_specs=[blk, blk, blk,
                  pl.BlockSpec(bias.shape, lambda b: (0, 0, 0, 0), pipeline_mode=pl.Buffered(1))],
        out_specs=blk,
        out_shape=jax.ShapeDtypeStruct(q.shape, BF16),
        scratch_shapes=[
            pltpu.VMEM((2, HEAD_PAIRS, 2 * GRID_W, nkeys), F32),
            pltpu.VMEM((2, HEAD_PAIRS, 2 * GRID_W, nkeys), BF16),
            pltpu.VMEM((2, HEAD_PAIRS, 2 * GRID_W, LANES), F32),
        ],
        compiler_params=pltpu.CompilerParams(
            dimension_semantics=("arbitrary",), vmem_limit_bytes=VMEM_LIMIT_BYTES),
        name="attn",
    )(q, k, v, bias)


def _pool_minus_token(ext, first_token, seq):
    n = ext.shape[0]
    half = POOL_HALO // 2
    tq = n - 2 * half

    def ahead(e, k):
        return pltpu.roll(e, n - k, axis=0)

    def window_count(t, w):
        return (jnp.minimum(t + w // 2, seq) - jnp.maximum(t - w // 2, 0)).astype(F32)

    edge = lax.broadcasted_iota(jnp.int32, (half, 1), 0)
    t_top = first_token + edge
    t_bot = first_token + tq - half + edge
    out = []
    for g, w in enumerate(POOL_WINDOWS):
        e = ext[:, g * POOL_GROUP_DIM:(g + 1) * POOL_GROUP_DIM]
        tok = e[half:half + tq]
        span = 1
        while 2 * span < w:
            e = e + ahead(e, span)
            span *= 2
        lo = half - w // 2
        first = ahead(e, lo)[:tq] if lo else e[:tq]
        s = first + e[half:half + tq]
        mean = jnp.concatenate([s[:half] / window_count(t_top, w),
                                s[half:tq - half] * (1.0 / w),
                                s[tq - half:] / window_count(t_bot, w)], axis=0)
        out.append(mean - tok)
    return jnp.concatenate(out, axis=-1)


def _tail_kernel(x_ref, a_ref, u_ref, up_ref, un_ref, wpool_ref, pscale_ref, wout_ref,
                 g2_ref, wg_ref, wu_ref, wd_ref, o_ref, *, seq, blocks_per_seq):
    i = pl.program_id(0) % blocks_per_seq
    tq = x_ref.shape[0]
    half = POOL_HALO // 2

    x1 = x_ref[...] + jnp.dot(a_ref[...], wout_ref[:NA_WIDTH, :], preferred_element_type=F32)

    prev_ok = (i > 0).astype(F32)
    next_ok = (i < blocks_per_seq - 1).astype(F32)
    ext = jnp.concatenate([up_ref[half:, :] * prev_ok, u_ref[...],
                           un_ref[:half, :] * next_ok], axis=0)
    d = _pool_minus_token(ext, i * tq, seq).astype(BF16)
    y = jnp.dot(d, wpool_ref[...], preferred_element_type=F32) * pscale_ref[...]
    x1 = x1 + jnp.dot(y.astype(BF16), wout_ref[NA_WIDTH:, :], preferred_element_type=F32)

    h = (x1 * _rms_scale(x1) * g2_ref[...]).astype(BF16)
    gate = jnp.dot(h, wg_ref[...], preferred_element_type=F32)
    up = jnp.dot(h, wu_ref[...], preferred_element_type=F32)
    act = (gate * jax.nn.sigmoid(gate) * up).astype(BF16)
    o_ref[...] = x1 + jnp.dot(act, wd_ref[...], preferred_element_type=F32)


def _tail_call(x2, a, u, wpool_bd, pscale, w_out, g2, w_gate, w_up, w_down, seq):
    n = x2.shape[0]
    tq = TAIL_BLOCK
    d_ff = w_gate.shape[1]
    nblk = seq // tq
    halo_per_blk = tq // POOL_HALO
    n_halo = seq // POOL_HALO
    tok = lambda g: (g, 0)
    const = lambda g: (0, 0)
    prev_halo = lambda g: ((g // nblk) * n_halo + jnp.maximum((g % nblk) * halo_per_blk - 1, 0), 0)
    next_halo = lambda g: ((g // nblk) * n_halo
                           + jnp.minimum((g % nblk + 1) * halo_per_blk, n_halo - 1), 0)
    resident = pl.Buffered(1)
    kernel = functools.partial(_tail_kernel, seq=seq, blocks_per_seq=nblk)
    return pl.pallas_call(
        kernel,
        grid=(n // tq,),
        in_specs=[
            pl.BlockSpec((tq, D_MODEL), tok),
            pl.BlockSpec((tq, NA_WIDTH), tok),
            pl.BlockSpec((tq, POOL_WIDTH), tok),
            pl.BlockSpec((POOL_HALO, POOL_WIDTH), prev_halo),
            pl.BlockSpec((POOL_HALO, POOL_WIDTH), next_halo),
            pl.BlockSpec((POOL_WIDTH, POOL_WIDTH), const, pipeline_mode=resident),
            pl.BlockSpec((1, POOL_WIDTH), const),
            pl.BlockSpec((D_MODEL, D_MODEL), const, pipeline_mode=resident),
            pl.BlockSpec((1, D_MODEL), const),
            pl.BlockSpec((D_MODEL, d_ff), const, pipeline_mode=resident),
            pl.BlockSpec((D_MODEL, d_ff), const, pipeline_mode=resident),
            pl.BlockSpec((d_ff, D_MODEL), const, pipeline_mode=resident),
        ],
        out_specs=pl.BlockSpec((tq, D_MODEL), tok),
        out_shape=jax.ShapeDtypeStruct(x2.shape, F32),
        compiler_params=pltpu.CompilerParams(
            dimension_semantics=("arbitrary",), vmem_limit_bytes=VMEM_LIMIT_BYTES),
        name="tail",
    )(x2, a, u, u, u, wpool_bd, pscale, w_out, g2, w_gate, w_up, w_down)


def _bias_table(rpb, rows):
    wr = min(WIN_ROWS_MAX, rows)
    ncls = wr
    c = np.arange(GRID_W)
    col_start = np.clip(c - WIN_COLS // 2, 0, GRID_W - WIN_COLS)
    kc = np.arange(GRID_W)
    inside = (kc[None, :] >= col_start[:, None]) & (kc[None, :] < col_start[:, None] + WIN_COLS)
    col_rel = kc[None, :] - c[:, None] + WIN_COLS - 1
    onehot = (col_rel[:, :, None] == np.arange(2 * WIN_COLS - 1)) & inside[:, :, None]
    cols = jnp.einsum('hrj,ckj->hrck', rpb.astype(F32), jnp.asarray(onehot, F32),
                      precision=lax.Precision.HIGHEST)
    cols = jnp.where(inside[None, None], cols, MASK_VALUE)
    per_cls = [cols[:, WIN_ROWS_MAX - 1 - s:WIN_ROWS_MAX - 1 - s + wr] for s in range(ncls)]
    tbl = jnp.stack(per_cls, axis=0)
    tbl = jnp.transpose(tbl, (0, 1, 3, 2, 4))
    return tbl.reshape(ncls, HEAD_PAIRS, 2 * GRID_W, wr * GRID_W)


def _block_diag(w_pool):
    g, d, _ = w_pool.shape
    eye = jnp.eye(g, dtype=w_pool.dtype)
    return jnp.einsum('gcd,gh->gchd', w_pool, eye).reshape(g * d, g * d)


def kernel(x, norm1_g, w_in, q_norm_g, k_norm_g, rpb, w_pool, pool_scale, w_out,
           norm2_g, w_gate, w_up, w_down):
    batch, seq, _ = x.shape
    rows = seq // GRID_W
    depth = w_in.shape[0]
    seg = jnp.asarray(np.kron(np.eye(NA_HEADS), np.full((NA_HEAD_DIM, NA_HEAD_DIM), 1.0 / NA_HEAD_DIM)),
                      dtype=BF16)
    x2 = x.reshape(batch * seq, D_MODEL)
    for l in range(depth):
        gq = jnp.tile(q_norm_g[l], NA_HEADS)[None, :]
        gk = jnp.tile(k_norm_g[l], NA_HEADS)[None, :]
        q, k, v, u = _proj_call(x2, norm1_g[l][None, :], w_in[l].astype(BF16), gq, gk, seg)
        a = _attn_call(q, k, v, _bias_table(rpb[l], rows), batch, seq)
        x2 = _tail_call(x2, a, u, _block_diag(w_pool[l]).astype(BF16), pool_scale[l][None, :],
                        w_out[l].astype(BF16), norm2_g[l][None, :], w_gate[l].astype(BF16),
                        w_up[l].astype(BF16), w_down[l].astype(BF16), seq)
    return x2.reshape(batch, seq, D_MODEL)
```

```python
import functools

import jax
import jax.numpy as jnp
import numpy as np
from jax import lax
from jax.experimental import pallas as pl
from jax.experimental.pallas import tpu as pltpu

D_MODEL = 1024
GRID_W = 64
NA_WIDTH = D_MODEL // 2
NA_HEADS = 8
NA_HEAD_DIM = NA_WIDTH // NA_HEADS
WIN_ROWS_MAX = 8
WIN_COLS = 16
POOL_WIDTH = D_MODEL - NA_WIDTH
POOL_WINDOWS = (2, 4, 8, 16)
POOL_GROUPS = len(POOL_WINDOWS)
POOL_GROUP_DIM = POOL_WIDTH // POOL_GROUPS
IN_WIDTH = 3 * NA_WIDTH + POOL_WIDTH
EPS = 1e-6

LANES = 128
HEAD_PAIRS = NA_WIDTH // LANES
MASK_VALUE = -1e30
LOG2_E = 1.4426950408889634
SCORE_SCALE = NA_HEAD_DIM ** -0.5 * LOG2_E
POOL_HALO = 16
VMEM_LIMIT_BYTES = 56 * 1024 * 1024

PROJ_BLOCK = 512
ATTN_ROWS_PER_STEP = 4
TAIL_BLOCK = 512

F32 = jnp.float32
BF16 = jnp.bfloat16


def _rms_scale(x):
    return lax.rsqrt(jnp.mean(x * x, axis=-1, keepdims=True) + EPS)


def _proj_kernel(x_ref, g1_ref, w_ref, gq_ref, gk_ref,
                 q_ref, k_ref, v_ref, u_ref):
    x = x_ref[...]
    h = (x * _rms_scale(x) * g1_ref[...]).astype(BF16)
    proj = jnp.dot(h, w_ref[...], preferred_element_type=F32)
    first_head = lax.broadcasted_iota(jnp.int32, (1, LANES), 1) < NA_HEAD_DIM

    def head_norm(t, g):
        tiles = []
        for p in range(HEAD_PAIRS):
            tp = t[:, p * LANES:(p + 1) * LANES]
            sq = tp * tp
            ss_first = jnp.sum(jnp.where(first_head, sq, 0.0), axis=-1, keepdims=True)
            ss_second = jnp.sum(jnp.where(first_head, 0.0, sq), axis=-1, keepdims=True)
            ms = jnp.where(first_head, ss_first, ss_second) * (1.0 / NA_HEAD_DIM)
            tiles.append(tp * lax.rsqrt(ms + EPS))
        return jnp.concatenate(tiles, axis=-1) * g

    q = head_norm(proj[:, :NA_WIDTH], gq_ref[...])
    k = head_norm(proj[:, NA_WIDTH:2 * NA_WIDTH], gk_ref[...])
    q_ref[...] = (q * SCORE_SCALE).astype(BF16)
    k_ref[...] = k.astype(BF16)
    v_ref[...] = proj[:, 2 * NA_WIDTH:3 * NA_WIDTH].astype(BF16)
    u_ref[...] = proj[:, 3 * NA_WIDTH:]


def _proj_call(x2, g1, w_in, gq, gk):
    n = x2.shape[0]
    tm = PROJ_BLOCK
    const = lambda i: (0, 0)
    row = lambda i: (i, 0)
    return pl.pallas_call(
        _proj_kernel,
        grid=(n // tm,),
        in_specs=[
            pl.BlockSpec((tm, D_MODEL), row),
            pl.BlockSpec((1, D_MODEL), const),
            pl.BlockSpec((D_MODEL, IN_WIDTH), const),
            pl.BlockSpec((1, NA_WIDTH), const),
            pl.BlockSpec((1, NA_WIDTH), const),
        ],
        out_specs=[
            pl.BlockSpec((tm, NA_WIDTH), row),
            pl.BlockSpec((tm, NA_WIDTH), row),
            pl.BlockSpec((tm, NA_WIDTH), row),
            pl.BlockSpec((tm, POOL_WIDTH), row),
        ],
        out_shape=[
            jax.ShapeDtypeStruct((n, NA_WIDTH), BF16),
            jax.ShapeDtypeStruct((n, NA_WIDTH), BF16),
            jax.ShapeDtypeStruct((n, NA_WIDTH), BF16),
            jax.ShapeDtypeStruct((n, POOL_WIDTH), F32),
        ],
        compiler_params=pltpu.CompilerParams(
            dimension_semantics=("arbitrary",), vmem_limit_bytes=VMEM_LIMIT_BYTES),
        name="proj",
    )(x2, g1, w_in, gq, gk)


def _attn_kernel(q_ref, k_ref, v_ref, bias_ref, a_ref, s_scr, m_scr, p_scr, l_scr, *, rows):
    wr = min(WIN_ROWS_MAX, rows)
    nkeys = wr * GRID_W
    lane = lax.broadcasted_iota(jnp.int32, (GRID_W, LANES), 1)
    first_head = lane < NA_HEAD_DIM
    pair_lanes = [slice(p * LANES, (p + 1) * LANES) for p in range(HEAD_PAIRS)]

    def window_start(r):
        return jnp.clip(r - wr // 2, 0, rows - wr)

    def scores(r, slot):
        rs = window_start(r)
        cls = r - rs
        q0 = pl.multiple_of(r * GRID_W, GRID_W)
        k0 = pl.multiple_of(rs * GRID_W, GRID_W)
        for pair, sl in enumerate(pair_lanes):
            qs = q_ref[pl.ds(q0, GRID_W), sl]
            zero = jnp.zeros_like(qs)
            lhs = jnp.concatenate([jnp.where(first_head, qs, zero),
                                   jnp.where(first_head, zero, qs)], axis=0)
            kw = k_ref[pl.ds(k0, nkeys), sl]
            s = lax.dot_general(lhs, kw, (((1,), (1,)), ((), ())),
                                preferred_element_type=F32)
            s = s + bias_ref[cls, pair]
            s_scr[slot, pair] = s
            m_scr[slot, pair] = jnp.broadcast_to(jnp.max(s, axis=-1, keepdims=True),
                                                 (2 * GRID_W, LANES))

    def softmax(slot):
        for pair in range(HEAD_PAIRS):
            m = m_scr[slot, pair]
            l = jnp.zeros((2 * GRID_W, LANES), F32)
            for j in range(nkeys // LANES):
                p = jnp.exp2(s_scr[slot, pair, :, j * LANES:(j + 1) * LANES] - m)
                p_scr[slot, pair, :, j * LANES:(j + 1) * LANES] = p.astype(BF16)
                l = l + p
            l_scr[slot, pair] = jnp.broadcast_to(jnp.sum(l, axis=-1, keepdims=True),
                                                 (2 * GRID_W, LANES))

    def weighted_values(r, slot):
        q0 = pl.multiple_of(r * GRID_W, GRID_W)
        k0 = pl.multiple_of(window_start(r) * GRID_W, GRID_W)
        for pair, sl in enumerate(pair_lanes):
            vw = v_ref[pl.ds(k0, nkeys), sl]
            o = jnp.dot(p_scr[slot, pair], vw, preferred_element_type=F32) / l_scr[slot, pair]
            oc = jnp.where(first_head, o[:GRID_W], o[GRID_W:])
            a_ref[pl.ds(q0, GRID_W), sl] = oc.astype(BF16)

    scores(0, 0)
    scores(1, 1)
    softmax(0)

    def row_group(j, carry):
        for k in range(ATTN_ROWS_PER_STEP):
            r = ATTN_ROWS_PER_STEP * j + k
            scores(jnp.minimum(r + 2, rows - 1), k % 2)
            softmax((k + 1) % 2)
            weighted_values(r, k % 2)
        return carry

    lax.fori_loop(0, rows // ATTN_ROWS_PER_STEP, row_group, 0)


def _attn_call(q, k, v, bias, batch, seq):
    rows = seq // GRID_W
    nkeys = min(WIN_ROWS_MAX, rows) * GRID_W
    per_batch = lambda b: (b, 0)
    blk = pl.BlockSpec((seq, NA_WIDTH), per_batch)
    kernel = functools.partial(_attn_kernel, rows=rows)
    return pl.pallas_call(
        kernel,
        grid=(batch,),
        in_specs=[blk, blk, blk,
                  pl.BlockSpec(bias.shape, lambda b: (0, 0, 0, 0), pipeline_mode=pl.Buffered(1))],
        out_specs=blk,
        out_shape=jax.ShapeDtypeStruct(q.shape, BF16),
        scratch_shapes=[
            pltpu.VMEM((2, HEAD_PAIRS, 2 * GRID_W, nkeys), F32),
            pltpu.VMEM((2, HEAD_PAIRS, 2 * GRID_W, LANES), F32),
            pltpu.VMEM((2, HEAD_PAIRS, 2 * GRID_W, nkeys), BF16),
            pltpu.VMEM((2, HEAD_PAIRS, 2 * GRID_W, LANES), F32),
        ],
        compiler_params=pltpu.CompilerParams(
            dimension_semantics=("arbitrary",), vmem_limit_bytes=VMEM_LIMIT_BYTES),
        name="attn",
    )(q, k, v, bias)


def _pool_minus_token(ext, first_token, seq):
    n = ext.shape[0]
    half = POOL_HALO // 2
    tq = n - 2 * half

    def ahead(e, k):
        return pltpu.roll(e, n - k, axis=0)

    def window_count(t, w):
        return (jnp.minimum(t + w // 2, seq) - jnp.maximum(t - w // 2, 0)).astype(F32)

    edge = lax.broadcasted_iota(jnp.int32, (half, 1), 0)
    t_top = first_token + edge
    t_bot = first_token + tq - half + edge
    out = []
    for g, w in enumerate(POOL_WINDOWS):
        e = ext[:, g * POOL_GROUP_DIM:(g + 1) * POOL_GROUP_DIM]
        tok = e[half:half + tq]
        span = 1
        while 2 * span < w:
            e = e + ahead(e, span)
            span *= 2
        lo = half - w // 2
        first = ahead(e, lo)[:tq] if lo else e[:tq]
        s = first + e[half:half + tq]
        mean = jnp.concatenate([s[:half] / window_count(t_top, w),
                                s[half:tq - half] * (1.0 / w),
                                s[tq - half:] / window_count(t_bot, w)], axis=0)
        out.append(mean - tok)
    return jnp.concatenate(out, axis=-1)


def _tail_kernel(x_ref, a_ref, u_ref, up_ref, un_ref, wpool_ref, pscale_ref, wout_ref,
                 g2_ref, wg_ref, wu_ref, wd_ref, o_ref, *, seq, blocks_per_seq):
    i = pl.program_id(0) % blocks_per_seq
    tq = x_ref.shape[0]
    half = POOL_HALO // 2

    x1 = x_ref[...] + jnp.dot(a_ref[...], wout_ref[:NA_WIDTH, :], preferred_element_type=F32)

    prev_ok = (i > 0).astype(F32)
    next_ok = (i < blocks_per_seq - 1).astype(F32)
    ext = jnp.concatenate([up_ref[half:, :] * prev_ok, u_ref[...],
                           un_ref[:half, :] * next_ok], axis=0)
    d = _pool_minus_token(ext, i * tq, seq).astype(BF16)
    y = jnp.dot(d, wpool_ref[...], preferred_element_type=F32) * pscale_ref[...]
    x1 = x1 + jnp.dot(y.astype(BF16), wout_ref[NA_WIDTH:, :], preferred_element_type=F32)

    h = (x1 * _rms_scale(x1) * g2_ref[...]).astype(BF16)
    gate = jnp.dot(h, wg_ref[...], preferred_element_type=F32)
    up = jnp.dot(h, wu_ref[...], preferred_element_type=F32)
    act = (gate * jax.nn.sigmoid(gate) * up).astype(BF16)
    o_ref[...] = x1 + jnp.dot(act, wd_ref[...], preferred_element_type=F32)


def _tail_call(x2, a, u, wpool_bd, pscale, w_out, g2, w_gate, w_up, w_down, seq):
    n = x2.shape[0]
    tq = TAIL_BLOCK
    d_ff = w_gate.shape[1]
    nblk = seq // tq
    halo_per_blk = tq // POOL_HALO
    n_halo = seq // POOL_HALO
    tok = lambda g: (g, 0)
    const = lambda g: (0, 0)
    prev_halo = lambda g: ((g // nblk) * n_halo + jnp.maximum((g % nblk) * halo_per_blk - 1, 0), 0)
    next_halo = lambda g: ((g // nblk) * n_halo
                           + jnp.minimum((g % nblk + 1) * halo_per_blk, n_halo - 1), 0)
    resident = pl.Buffered(1)
    kernel = functools.partial(_tail_kernel, seq=seq, blocks_per_seq=nblk)
    return pl.pallas_call(
        kernel,
        grid=(n // tq,),
        in_specs=[
            pl.BlockSpec((tq, D_MODEL), tok),
            pl.BlockSpec((tq, NA_WIDTH), tok),
            pl.BlockSpec((tq, POOL_WIDTH), tok),
            pl.BlockSpec((POOL_HALO, POOL_WIDTH), prev_halo),
            pl.BlockSpec((POOL_HALO, POOL_WIDTH), next_halo),
            pl.BlockSpec((POOL_WIDTH, POOL_WIDTH), const, pipeline_mode=resident),
            pl.BlockSpec((1, POOL_WIDTH), const),
            pl.BlockSpec((D_MODEL, D_MODEL), const, pipeline_mode=resident),
            pl.BlockSpec((1, D_MODEL), const),
            pl.BlockSpec((D_MODEL, d_ff), const, pipeline_mode=resident),
            pl.BlockSpec((D_MODEL, d_ff), const, pipeline_mode=resident),
            pl.BlockSpec((d_ff, D_MODEL), const, pipeline_mode=resident),
        ],
        out_specs=pl.BlockSpec((tq, D_MODEL), tok),
        out_shape=jax.ShapeDtypeStruct(x2.shape, F32),
        compiler_params=pltpu.CompilerParams(
            dimension_semantics=("arbitrary",), vmem_limit_bytes=VMEM_LIMIT_BYTES),
        name="tail",
    )(x2, a, u, u, u, wpool_bd, pscale, w_out, g2, w_gate, w_up, w_down)


def _bias_table(rpb, rows):
    wr = min(WIN_ROWS_MAX, rows)
    ncls = wr
    c = np.arange(GRID_W)
    col_start = np.clip(c - WIN_COLS // 2, 0, GRID_W - WIN_COLS)
    kc = np.arange(GRID_W)
    inside = (kc[None, :] >= col_start[:, None]) & (kc[None, :] < col_start[:, None] + WIN_COLS)
    col_rel = kc[None, :] - c[:, None] + WIN_COLS - 1
    onehot = (col_rel[:, :, None] == np.arange(2 * WIN_COLS - 1)) & inside[:, :, None]
    cols = jnp.einsum('hrj,ckj->hrck', rpb.astype(F32), jnp.asarray(onehot, F32),
                      precision=lax.Precision.HIGHEST)
    cols = jnp.where(inside[None, None], cols, MASK_VALUE)
    per_cls = [cols[:, WIN_ROWS_MAX - 1 - s:WIN_ROWS_MAX - 1 - s + wr] for s in range(ncls)]
    tbl = jnp.stack(per_cls, axis=0)
    tbl = jnp.transpose(tbl, (0, 1, 3, 2, 4))
    return tbl.reshape(ncls, HEAD_PAIRS, 2 * GRID_W, wr * GRID_W)


def _block_diag(w_pool):
    g, d, _ = w_pool.shape
    eye = jnp.eye(g, dtype=w_pool.dtype)
    return jnp.einsum('gcd,gh->gchd', w_pool, eye).reshape(g * d, g * d)


def kernel(x, norm1_g, w_in, q_norm_g, k_norm_g, rpb, w_pool, pool_scale, w_out,
           norm2_g, w_gate, w_up, w_down):
    batch, seq, _ = x.shape
    rows = seq // GRID_W
    depth = w_in.shape[0]
    x2 = x.reshape(batch * seq, D_MODEL)
    for l in range(depth):
        gq = jnp.tile(q_norm_g[l], NA_HEADS)[None, :]
        gk = jnp.tile(k_norm_g[l], NA_HEADS)[None, :]
        q, k, v, u = _proj_call(x2, norm1_g[l][None, :], w_in[l].astype(BF16), gq, gk)
        a = _attn_call(q, k, v, _bias_table(rpb[l] * LOG2_E, rows), batch, seq)
        x2 = _tail_call(x2, a, u, _block_diag(w_pool[l]).astype(BF16), pool_scale[l][None, :],
                        w_out[l].astype(BF16), norm2_g[l][None, :], w_gate[l].astype(BF16),
                        w_up[l].astype(BF16), w_down[l].astype(BF16), seq)
    return x2.reshape(batch, seq, D_MODEL)
```

```python
import functools

import jax
import jax.numpy as jnp
import numpy as np
from jax import lax
from jax.experimental import pallas as pl
from jax.experimental.pallas import tpu as pltpu

D_MODEL = 1024
GRID_W = 64
NA_WIDTH = D_MODEL // 2
NA_HEADS = 8
NA_HEAD_DIM = NA_WIDTH // NA_HEADS
WIN_ROWS_MAX = 8
WIN_COLS = 16
POOL_WIDTH = D_MODEL - NA_WIDTH
POOL_WINDOWS = (2, 4, 8, 16)
POOL_GROUPS = len(POOL_WINDOWS)
POOL_GROUP_DIM = POOL_WIDTH // POOL_GROUPS
IN_WIDTH = 3 * NA_WIDTH + POOL_WIDTH
EPS = 1e-6

LANES = 128
HEAD_PAIRS = NA_WIDTH // LANES
MASK_VALUE = -1e30
LOG2_E = 1.4426950408889634
SCORE_SCALE = NA_HEAD_DIM ** -0.5 * LOG2_E
POOL_HALO = 16
VMEM_LIMIT_BYTES = 56 * 1024 * 1024

PROJ_BLOCK = 1024
ATTN_ROWS_PER_STEP = 4
FFN_CHUNK = 1024
TAIL_BLOCK = 1024

F32 = jnp.float32
BF16 = jnp.bfloat16


def _rms_scale(x):
    return lax.rsqrt(jnp.mean(x * x, axis=-1, keepdims=True) + EPS)


def _proj_kernel(x_ref, g1_ref, w_ref, gq_ref, gk_ref,
                 q_ref, k_ref, v_ref, u_ref, w_bf):
    @pl.when(pl.program_id(0) == 0)
    def _():
        w_bf[...] = w_ref[...].astype(BF16)

    x = x_ref[...]
    h = (x * _rms_scale(x) * g1_ref[...]).astype(BF16)
    proj = jnp.dot(h, w_bf[...], preferred_element_type=F32)
    first_head = lax.broadcasted_iota(jnp.int32, (1, LANES), 1) < NA_HEAD_DIM

    def head_norm(t, g):
        tiles = []
        for p in range(HEAD_PAIRS):
            tp = t[:, p * LANES:(p + 1) * LANES]
            sq = tp * tp
            ss_first = jnp.sum(jnp.where(first_head, sq, 0.0), axis=-1, keepdims=True)
            ss_second = jnp.sum(jnp.where(first_head, 0.0, sq), axis=-1, keepdims=True)
            ms = jnp.where(first_head, ss_first, ss_second) * (1.0 / NA_HEAD_DIM)
            tiles.append(tp * lax.rsqrt(ms + EPS))
        return jnp.concatenate(tiles, axis=-1) * g

    q = head_norm(proj[:, :NA_WIDTH], gq_ref[...])
    k = head_norm(proj[:, NA_WIDTH:2 * NA_WIDTH], gk_ref[...])
    q_ref[...] = (q * SCORE_SCALE).astype(BF16)
    k_ref[...] = k.astype(BF16)
    v_ref[...] = proj[:, 2 * NA_WIDTH:3 * NA_WIDTH].astype(BF16)
    u_ref[...] = proj[:, 3 * NA_WIDTH:]


def _proj_call(x2, g1, w_in, layer, gq, gk):
    n = x2.shape[0]
    tm = PROJ_BLOCK
    const = lambda i: (0, 0)
    row = lambda i: (i, 0)
    return pl.pallas_call(
        _proj_kernel,
        grid=(n // tm,),
        in_specs=[
            pl.BlockSpec((tm, D_MODEL), row),
            pl.BlockSpec((1, D_MODEL), const),
            pl.BlockSpec((None, D_MODEL, IN_WIDTH), lambda i: (layer, 0, 0),
                         pipeline_mode=pl.Buffered(1)),
            pl.BlockSpec((1, NA_WIDTH), const),
            pl.BlockSpec((1, NA_WIDTH), const),
        ],
        out_specs=[
            pl.BlockSpec((tm, NA_WIDTH), row),
            pl.BlockSpec((tm, NA_WIDTH), row),
            pl.BlockSpec((tm, NA_WIDTH), row),
            pl.BlockSpec((tm, POOL_WIDTH), row),
        ],
        out_shape=[
            jax.ShapeDtypeStruct((n, NA_WIDTH), BF16),
            jax.ShapeDtypeStruct((n, NA_WIDTH), BF16),
            jax.ShapeDtypeStruct((n, NA_WIDTH), BF16),
            jax.ShapeDtypeStruct((n, POOL_WIDTH), F32),
        ],
        scratch_shapes=[pltpu.VMEM((D_MODEL, IN_WIDTH), BF16)],
        compiler_params=pltpu.CompilerParams(
            dimension_semantics=("arbitrary",), vmem_limit_bytes=VMEM_LIMIT_BYTES),
        name="proj",
    )(x2, g1, w_in, gq, gk)


def _attn_kernel(q_ref, k_ref, v_ref, bias_ref, a_ref, s_scr, m_scr, p_scr, *, rows):
    wr = WIN_ROWS_MAX
    nkeys = wr * GRID_W
    lane = lax.broadcasted_iota(jnp.int32, (GRID_W, LANES), 1)
    first_head = lane < NA_HEAD_DIM
    pair_lanes = [slice(p * LANES, (p + 1) * LANES) for p in range(HEAD_PAIRS)]

    def window_start(r):
        return jnp.clip(r - wr // 2, 0, rows - wr)

    def scores(r, slot):
        rs = window_start(r)
        first_rel = rs - r + WIN_ROWS_MAX - 1
        q0 = pl.multiple_of(r * GRID_W, GRID_W)
        k0 = pl.multiple_of(rs * GRID_W, GRID_W)
        for pair, sl in enumerate(pair_lanes):
            qs = q_ref[pl.ds(q0, GRID_W), sl]
            zero = jnp.zeros_like(qs)
            lhs = jnp.concatenate([jnp.where(first_head, qs, zero),
                                   jnp.where(first_head, zero, qs)], axis=0)
            kw = k_ref[pl.ds(k0, nkeys), sl]
            s = lax.dot_general(lhs, kw, (((1,), (1,)), ((), ())),
                                preferred_element_type=F32)
            s = s + jnp.concatenate(
                [jnp.concatenate([bias_ref[2 * pair + hh, first_rel + 2 * j] for hh in range(2)], axis=0)
                 for j in range(nkeys // LANES)], axis=1)
            s_scr[slot, pair] = s
            m_scr[slot, pair] = jnp.broadcast_to(jnp.max(s, axis=-1, keepdims=True),
                                                 (2 * GRID_W, LANES))

    def softmax(slot):
        for pair in range(HEAD_PAIRS):
            m = m_scr[slot, pair]
            for j in range(nkeys // LANES):
                p = jnp.exp2(s_scr[slot, pair, :, j * LANES:(j + 1) * LANES] - m)
                p_scr[slot, pair, :, j * LANES:(j + 1) * LANES] = p.astype(BF16)

    ones = jnp.ones((nkeys, LANES), BF16)

    def weighted_values(r, slot):
        q0 = pl.multiple_of(r * GRID_W, GRID_W)
        k0 = pl.multiple_of(window_start(r) * GRID_W, GRID_W)
        for pair, sl in enumerate(pair_lanes):
            vw = jnp.concatenate([v_ref[pl.ds(k0, nkeys), sl], ones], axis=1)
            ol = jnp.dot(p_scr[slot, pair], vw, preferred_element_type=F32)
            o = ol[:, :LANES] / ol[:, LANES:]
            oc = jnp.where(first_head, o[:GRID_W], o[GRID_W:])
            a_ref[pl.ds(q0, GRID_W), sl] = oc.astype(BF16)

    scores(0, 0)
    scores(1, 1)
    softmax(0)

    def row_group(j, carry):
        for k in range(ATTN_ROWS_PER_STEP):
            r = ATTN_ROWS_PER_STEP * j + k
            scores(jnp.minimum(r + 2, rows - 1), k % 2)
            softmax((k + 1) % 2)
            weighted_values(r, k % 2)
        return carry

    lax.fori_loop(0, rows // ATTN_ROWS_PER_STEP, row_group, 0)


def _attn_call(q, k, v, bias, batch, seq):
    rows = seq // GRID_W
    assert rows >= WIN_ROWS_MAX and rows % ATTN_ROWS_PER_STEP == 0
    nkeys = WIN_ROWS_MAX * GRID_W
    per_batch = lambda b: (b, 0)
    blk = pl.BlockSpec((seq, NA_WIDTH), per_batch)
    kernel = functools.partial(_attn_kernel, rows=rows)
    return pl.pallas_call(
        kernel,
        grid=(batch,),
        in_specs=[blk, blk, blk,
                  pl.BlockSpec(bias.shape, lambda b: (0, 0, 0, 0), pipeline_mode=pl.Buffered(1))],
        out_specs=blk,
        out_shape=jax.ShapeDtypeStruct(q.shape, BF16),
        scratch_shapes=[
            pltpu.VMEM((2, HEAD_PAIRS, 2 * GRID_W, nkeys), F32),
            pltpu.VMEM((2, HEAD_PAIRS, 2 * GRID_W, LANES), F32),
            pltpu.VMEM((2, HEAD_PAIRS, 2 * GRID_W, nkeys), BF16),
        ],
        compiler_params=pltpu.CompilerParams(
            dimension_semantics=("arbitrary",), vmem_limit_bytes=VMEM_LIMIT_BYTES),
        name="attn",
    )(q, k, v, bias)


def _pool_minus_token(ext, first_token, seq):
    n = ext.shape[0]
    half = POOL_HALO // 2
    tq = n - 2 * half

    def ahead(e, k):
        return pltpu.roll(e, n - k, axis=0)

    def window_count(t, w):
        return (jnp.minimum(t + w // 2, seq) - jnp.maximum(t - w // 2, 0)).astype(F32)

    edge = lax.broadcasted_iota(jnp.int32, (half, 1), 0)
    t_top = first_token + edge
    t_bot = first_token + tq - half + edge
    out = []
    for g, w in enumerate(POOL_WINDOWS):
        e = ext[:, g * POOL_GROUP_DIM:(g + 1) * POOL_GROUP_DIM]
        tok = e[half:half + tq]
        span = 1
        while 2 * span < w:
            e = e + ahead(e, span)
            span *= 2
        lo = half - w // 2
        first = ahead(e, lo)[:tq] if lo else e[:tq]
        s = first + e[half:half + tq]
        mean = jnp.concatenate([s[:half] / window_count(t_top, w),
                                s[half:tq - half] * (1.0 / w),
                                s[tq - half:] / window_count(t_bot, w)], axis=0)
        out.append(mean - tok)
    return jnp.concatenate(out, axis=-1)


def _tail_kernel(x_ref, a_ref, u_ref, up_ref, un_ref, wpool_ref, pscale_ref, wout_ref,
                 g2_ref, wg_ref, wu_ref, wd_ref, o_ref, *, seq, blocks_per_seq):
    i = pl.program_id(0) % blocks_per_seq
    tq = x_ref.shape[0]
    half = POOL_HALO // 2

    x1 = x_ref[...] + jnp.dot(a_ref[...], wout_ref[:NA_WIDTH, :], preferred_element_type=F32)

    prev_ok = (i > 0).astype(F32)
    next_ok = (i < blocks_per_seq - 1).astype(F32)
    ext = jnp.concatenate([up_ref[half:, :] * prev_ok, u_ref[...],
                           un_ref[:half, :] * next_ok], axis=0)
    d = _pool_minus_token(ext, i * tq, seq).astype(BF16)
    y = jnp.dot(d, wpool_ref[...], preferred_element_type=F32) * pscale_ref[...]
    x1 = x1 + jnp.dot(y.astype(BF16), wout_ref[NA_WIDTH:, :], preferred_element_type=F32)

    h = (x1 * _rms_scale(x1) * g2_ref[...]).astype(BF16)
    d_ff = wd_ref.shape[0]
    ffn = None
    for f0 in range(0, d_ff, FFN_CHUNK):
        f1 = min(f0 + FFN_CHUNK, d_ff)
        gate = jnp.dot(h, wg_ref[:, f0:f1], preferred_element_type=F32)
        up = jnp.dot(h, wu_ref[:, f0:f1], preferred_element_type=F32)
        act = (gate * jax.nn.sigmoid(gate) * up).astype(BF16)
        part = jnp.dot(act, wd_ref[f0:f1, :], preferred_element_type=F32)
        ffn = part if ffn is None else ffn + part
    o_ref[...] = x1 + ffn


def _tail_call(x2, a, u, wpool_bd, pscale, w_out, g2, w_gate, w_up, w_down, seq):
    n = x2.shape[0]
    tq = TAIL_BLOCK
    d_ff = w_gate.shape[1]
    nblk = seq // tq
    halo_per_blk = tq // POOL_HALO
    n_halo = seq // POOL_HALO
    tok = lambda g: (g, 0)
    const = lambda g: (0, 0)
    prev_halo = lambda g: ((g // nblk) * n_halo + jnp.maximum((g % nblk) * halo_per_blk - 1, 0), 0)
    next_halo = lambda g: ((g // nblk) * n_halo
                           + jnp.minimum((g % nblk + 1) * halo_per_blk, n_halo - 1), 0)
    resident = pl.Buffered(1)
    kernel = functools.partial(_tail_kernel, seq=seq, blocks_per_seq=nblk)
    return pl.pallas_call(
        kernel,
        grid=(n // tq,),
        in_specs=[
            pl.BlockSpec((tq, D_MODEL), tok),
            pl.BlockSpec((tq, NA_WIDTH), tok),
            pl.BlockSpec((tq, POOL_WIDTH), tok),
            pl.BlockSpec((POOL_HALO, POOL_WIDTH), prev_halo),
            pl.BlockSpec((POOL_HALO, POOL_WIDTH), next_halo),
            pl.BlockSpec((POOL_WIDTH, POOL_WIDTH), const, pipeline_mode=resident),
            pl.BlockSpec((1, POOL_WIDTH), const),
            pl.BlockSpec((D_MODEL, D_MODEL), const, pipeline_mode=resident),
            pl.BlockSpec((1, D_MODEL), const),
            pl.BlockSpec((D_MODEL, d_ff), const, pipeline_mode=resident),
            pl.BlockSpec((D_MODEL, d_ff), const, pipeline_mode=resident),
            pl.BlockSpec((d_ff, D_MODEL), const, pipeline_mode=resident),
        ],
        out_specs=pl.BlockSpec((tq, D_MODEL), tok),
        out_shape=jax.ShapeDtypeStruct(x2.shape, F32),
        compiler_params=pltpu.CompilerParams(
            dimension_semantics=("arbitrary",), vmem_limit_bytes=VMEM_LIMIT_BYTES),
        name="tail",
    )(x2, a, u, u, u, wpool_bd, pscale, w_out, g2, w_gate, w_up, w_down)


def _bias_table(rpb):
    c = np.arange(GRID_W)
    col_start = np.clip(c - WIN_COLS // 2, 0, GRID_W - WIN_COLS)
    kc = np.arange(2 * GRID_W) % GRID_W
    inside = (kc[None, :] >= col_start[:, None]) & (kc[None, :] < col_start[:, None] + WIN_COLS)
    col_rel = kc[None, :] - c[:, None] + WIN_COLS - 1
    onehot = (col_rel[:, :, None] == np.arange(2 * WIN_COLS - 1)) & inside[:, :, None]
    both = jnp.einsum('hrj,cmj->hrcm', rpb.astype(F32), jnp.asarray(onehot, F32),
                      precision=lax.Precision.HIGHEST)
    low_lanes = np.arange(2 * GRID_W) < GRID_W
    tbl = jnp.where(low_lanes, both[:, :-1], both[:, 1:])
    return jnp.where(inside[None, None], tbl, MASK_VALUE)


def _block_diag(w_pool):
    g, d, _ = w_pool.shape
    eye = jnp.eye(g, dtype=w_pool.dtype)
    return jnp.einsum('gcd,gh->gchd', w_pool, eye).reshape(g * d, g * d)


def kernel(x, norm1_g, w_in, q_norm_g, k_norm_g, rpb, w_pool, pool_scale, w_out,
           norm2_g, w_gate, w_up, w_down):
    batch, seq, _ = x.shape
    depth = w_in.shape[0]
    x2 = x.reshape(batch * seq, D_MODEL)
    for l in range(depth):
        gq = jnp.tile(q_norm_g[l], NA_HEADS)[None, :]
        gk = jnp.tile(k_norm_g[l], NA_HEADS)[None, :]
        q, k, v, u = _proj_call(x2, norm1_g[l][None, :], w_in, l, gq, gk)
        a = _attn_call(q, k, v, _bias_table(rpb[l] * LOG2_E), batch, seq)
        x2 = _tail_call(x2, a, u, _block_diag(w_pool[l]).astype(BF16), pool_scale[l][None, :],
                        w_out[l].astype(BF16), norm2_g[l][None, :], w_gate[l].astype(BF16),
                        w_up[l].astype(BF16), w_down[l].astype(BF16), seq)
    return x2.reshape(batch, seq, D_MODEL)
```

```python
import functools

import jax
import jax.numpy as jnp
import numpy as np
from jax import lax
from jax.experimental import pallas as pl
from jax.experimental.pallas import tpu as pltpu

D_MODEL = 1024
GRID_W = 64
NA_WIDTH = D_MODEL // 2
NA_HEADS = 8
NA_HEAD_DIM = NA_WIDTH // NA_HEADS
WIN_ROWS_MAX = 8
WIN_COLS = 16
POOL_WIDTH = D_MODEL - NA_WIDTH
POOL_WINDOWS = (2, 4, 8, 16)
POOL_GROUPS = len(POOL_WINDOWS)
POOL_GROUP_DIM = POOL_WIDTH // POOL_GROUPS
IN_WIDTH = 3 * NA_WIDTH + POOL_WIDTH
EPS = 1e-6

LANES = 128
HEAD_PAIRS = NA_WIDTH // LANES
MASK_VALUE = -1e30
LOG2_E = 1.4426950408889634
SCORE_SCALE = NA_HEAD_DIM ** -0.5 * LOG2_E
POOL_HALO = 16
VMEM_LIMIT_BYTES = 56 * 1024 * 1024

PROJ_BLOCK = 1024
N_LATER_WEIGHTS = 4
CAST_BANDS = 16
BF16_SUBLANES = 16
ATTN_ROWS_PER_STEP = 4
FFN_CHUNK = 1024
TAIL_BLOCK = 1024

F32 = jnp.float32
BF16 = jnp.bfloat16


def _rms_scale(x):
    return lax.rsqrt(jnp.mean(x * x, axis=-1, keepdims=True) + EPS)


def _proj_kernel(x_ref, g1_ref, w_ref, gq_ref, gk_ref, *refs):
    later_f32 = refs[:N_LATER_WEIGHTS]
    q_ref, k_ref, v_ref, u_ref = refs[N_LATER_WEIGHTS:N_LATER_WEIGHTS + 4]
    later_bf = refs[N_LATER_WEIGHTS + 4:2 * N_LATER_WEIGHTS + 4]
    w_bf = refs[-1]

    @pl.when(pl.program_id(0) == 0)
    def _():
        w_bf[...] = w_ref[...].astype(BF16)

    for src, dst in zip(later_f32, later_bf):
        dst[...] = src[...].astype(BF16)

    x = x_ref[...]
    h = (x * _rms_scale(x) * g1_ref[...]).astype(BF16)
    proj = jnp.dot(h, w_bf[...], preferred_element_type=F32)
    first_head = lax.broadcasted_iota(jnp.int32, (1, LANES), 1) < NA_HEAD_DIM

    def head_norm(t, g):
        tiles = []
        for p in range(HEAD_PAIRS):
            tp = t[:, p * LANES:(p + 1) * LANES]
            sq = tp * tp
            ss_first = jnp.sum(jnp.where(first_head, sq, 0.0), axis=-1, keepdims=True)
            ss_second = jnp.sum(jnp.where(first_head, 0.0, sq), axis=-1, keepdims=True)
            ms = jnp.where(first_head, ss_first, ss_second) * (1.0 / NA_HEAD_DIM)
            tiles.append(tp * lax.rsqrt(ms + EPS))
        return jnp.concatenate(tiles, axis=-1) * g

    q = head_norm(proj[:, :NA_WIDTH], gq_ref[...])
    k = head_norm(proj[:, NA_WIDTH:2 * NA_WIDTH], gk_ref[...])
    q_ref[...] = (q * SCORE_SCALE).astype(BF16)
    k_ref[...] = k.astype(BF16)
    v_ref[...] = proj[:, 2 * NA_WIDTH:3 * NA_WIDTH].astype(BF16)
    u_ref[...] = proj[:, 3 * NA_WIDTH:]


def _proj_call(x2, g1, w_in, layer, gq, gk, later_weights):
    assert len(later_weights) == N_LATER_WEIGHTS
    n = x2.shape[0]
    tm = PROJ_BLOCK
    steps = n // tm
    assert steps % CAST_BANDS == 0
    steps_per_band = steps // CAST_BANDS
    const = lambda i: (0, 0)
    row = lambda i: (i, 0)
    band_in, band_out, band_shape = [], [], []
    for w in later_weights:
        rows, cols = w.shape[1:]
        assert rows % (CAST_BANDS * BF16_SUBLANES) == 0
        band_in.append(pl.BlockSpec((None, rows // CAST_BANDS, cols),
                                    lambda i: (layer, i // steps_per_band, 0)))
        band_out.append(pl.BlockSpec((rows // CAST_BANDS, cols), lambda i: (i // steps_per_band, 0)))
        band_shape.append(jax.ShapeDtypeStruct((rows, cols), BF16))
    outs = pl.pallas_call(
        _proj_kernel,
        grid=(steps,),
        in_specs=[
            pl.BlockSpec((tm, D_MODEL), row),
            pl.BlockSpec((1, D_MODEL), const),
            pl.BlockSpec((None, D_MODEL, IN_WIDTH), lambda i: (layer, 0, 0),
                         pipeline_mode=pl.Buffered(1)),
            pl.BlockSpec((1, NA_WIDTH), const),
            pl.BlockSpec((1, NA_WIDTH), const),
        ] + band_in,
        out_specs=[
            pl.BlockSpec((tm, NA_WIDTH), row),
            pl.BlockSpec((tm, NA_WIDTH), row),
            pl.BlockSpec((tm, NA_WIDTH), row),
            pl.BlockSpec((tm, POOL_WIDTH), row),
        ] + band_out,
        out_shape=[
            jax.ShapeDtypeStruct((n, NA_WIDTH), BF16),
            jax.ShapeDtypeStruct((n, NA_WIDTH), BF16),
            jax.ShapeDtypeStruct((n, NA_WIDTH), BF16),
            jax.ShapeDtypeStruct((n, POOL_WIDTH), F32),
        ] + band_shape,
        scratch_shapes=[pltpu.VMEM((D_MODEL, IN_WIDTH), BF16)],
        compiler_params=pltpu.CompilerParams(
            dimension_semantics=("arbitrary",), vmem_limit_bytes=VMEM_LIMIT_BYTES),
        name="proj",
    )(x2, g1, w_in, gq, gk, *later_weights)
    return outs[:4], outs[4:]


def _attn_kernel(q_ref, k_ref, v_ref, bias_ref, a_ref, s_scr, m_scr, p_scr, *, rows):
    wr = WIN_ROWS_MAX
    nkeys = wr * GRID_W
    lane = lax.broadcasted_iota(jnp.int32, (GRID_W, LANES), 1)
    first_head = lane < NA_HEAD_DIM
    pair_lanes = [slice(p * LANES, (p + 1) * LANES) for p in range(HEAD_PAIRS)]

    def window_start(r):
        return jnp.clip(r - wr // 2, 0, rows - wr)

    def scores(r, slot):
        rs = window_start(r)
        first_rel = rs - r + WIN_ROWS_MAX - 1
        q0 = pl.multiple_of(r * GRID_W, GRID_W)
        k0 = pl.multiple_of(rs * GRID_W, GRID_W)
        for pair, sl in enumerate(pair_lanes):
            qs = q_ref[pl.ds(q0, GRID_W), sl]
            zero = jnp.zeros_like(qs)
            lhs = jnp.concatenate([jnp.where(first_head, qs, zero),
                                   jnp.where(first_head, zero, qs)], axis=0)
            kw = k_ref[pl.ds(k0, nkeys), sl]
            s = lax.dot_general(lhs, kw, (((1,), (1,)), ((), ())),
                                preferred_element_type=F32)
            s = s + jnp.concatenate(
                [jnp.concatenate([bias_ref[2 * pair + hh, first_rel + 2 * j] for hh in range(2)], axis=0)
                 for j in range(nkeys // LANES)], axis=1)
            s_scr[slot, pair] = s
            m_scr[slot, pair] = jnp.broadcast_to(jnp.max(s, axis=-1, keepdims=True),
                                                 (2 * GRID_W, LANES))

    def softmax(slot):
        for pair in range(HEAD_PAIRS):
            m = m_scr[slot, pair]
            for j in range(nkeys // LANES):
                p = jnp.exp2(s_scr[slot, pair, :, j * LANES:(j + 1) * LANES] - m)
                p_scr[slot, pair, :, j * LANES:(j + 1) * LANES] = p.astype(BF16)

    ones = jnp.ones((nkeys, LANES), BF16)

    def weighted_values(r, slot):
        q0 = pl.multiple_of(r * GRID_W, GRID_W)
        k0 = pl.multiple_of(window_start(r) * GRID_W, GRID_W)
        for pair, sl in enumerate(pair_lanes):
            vw = jnp.concatenate([v_ref[pl.ds(k0, nkeys), sl], ones], axis=1)
            ol = jnp.dot(p_scr[slot, pair], vw, preferred_element_type=F32)
            o = ol[:, :LANES] / ol[:, LANES:]
            oc = jnp.where(first_head, o[:GRID_W], o[GRID_W:])
            a_ref[pl.ds(q0, GRID_W), sl] = oc.astype(BF16)

    scores(0, 0)
    scores(1, 1)
    softmax(0)

    def row_group(j, carry):
        for k in range(ATTN_ROWS_PER_STEP):
            r = ATTN_ROWS_PER_STEP * j + k
            scores(jnp.minimum(r + 2, rows - 1), k % 2)
            softmax((k + 1) % 2)
            weighted_values(r, k % 2)
        return carry

    lax.fori_loop(0, rows // ATTN_ROWS_PER_STEP, row_group, 0)


def _attn_call(q, k, v, bias, batch, seq):
    rows = seq // GRID_W
    assert rows >= WIN_ROWS_MAX and rows % ATTN_ROWS_PER_STEP == 0
    nkeys = WIN_ROWS_MAX * GRID_W
    per_batch = lambda b: (b, 0)
    blk = pl.BlockSpec((seq, NA_WIDTH), per_batch)
    kernel = functools.partial(_attn_kernel, rows=rows)
    return pl.pallas_call(
        kernel,
        grid=(batch,),
        in_specs=[blk, blk, blk,
                  pl.BlockSpec(bias.shape, lambda b: (0, 0, 0, 0), pipeline_mode=pl.Buffered(1))],
        out_specs=blk,
        out_shape=jax.ShapeDtypeStruct(q.shape, BF16),
        scratch_shapes=[
            pltpu.VMEM((2, HEAD_PAIRS, 2 * GRID_W, nkeys), F32),
            pltpu.VMEM((2, HEAD_PAIRS, 2 * GRID_W, LANES), F32),
            pltpu.VMEM((2, HEAD_PAIRS, 2 * GRID_W, nkeys), BF16),
        ],
        compiler_params=pltpu.CompilerParams(
            dimension_semantics=("arbitrary",), vmem_limit_bytes=VMEM_LIMIT_BYTES),
        name="attn",
    )(q, k, v, bias)


def _pool_minus_token(ext, first_token, seq):
    n = ext.shape[0]
    half = POOL_HALO // 2
    tq = n - 2 * half

    def ahead(e, k):
        return pltpu.roll(e, n - k, axis=0)

    def window_count(t, w):
        return (jnp.minimum(t + w // 2, seq) - jnp.maximum(t - w // 2, 0)).astype(F32)

    edge = lax.broadcasted_iota(jnp.int32, (half, 1), 0)
    t_top = first_token + edge
    t_bot = first_token + tq - half + edge
    out = []
    for g, w in enumerate(POOL_WINDOWS):
        e = ext[:, g * POOL_GROUP_DIM:(g + 1) * POOL_GROUP_DIM]
        tok = e[half:half + tq]
        span = 1
        while 2 * span < w:
            e = e + ahead(e, span)
            span *= 2
        lo = half - w // 2
        first = ahead(e, lo)[:tq] if lo else e[:tq]
        s = first + e[half:half + tq]
        mean = jnp.concatenate([s[:half] / window_count(t_top, w),
                                s[half:tq - half] * (1.0 / w),
                                s[tq - half:] / window_count(t_bot, w)], axis=0)
        out.append(mean - tok)
    return jnp.concatenate(out, axis=-1)


def _tail_kernel(x_ref, a_ref, u_ref, up_ref, un_ref, wpool_ref, pscale_ref, wout_ref,
                 g2_ref, wg_ref, wu_ref, wd_ref, o_ref, *, seq, blocks_per_seq):
    i = pl.program_id(0) % blocks_per_seq
    tq = x_ref.shape[0]
    half = POOL_HALO // 2

    x1 = x_ref[...] + jnp.dot(a_ref[...], wout_ref[:NA_WIDTH, :], preferred_element_type=F32)

    prev_ok = (i > 0).astype(F32)
    next_ok = (i < blocks_per_seq - 1).astype(F32)
    ext = jnp.concatenate([up_ref[half:, :] * prev_ok, u_ref[...],
                           un_ref[:half, :] * next_ok], axis=0)
    d = _pool_minus_token(ext, i * tq, seq).astype(BF16)
    y = jnp.dot(d, wpool_ref[...], preferred_element_type=F32) * pscale_ref[...]
    x1 = x1 + jnp.dot(y.astype(BF16), wout_ref[NA_WIDTH:, :], preferred_element_type=F32)

    h = (x1 * _rms_scale(x1) * g2_ref[...]).astype(BF16)
    d_ff = wd_ref.shape[0]
    ffn = None
    for f0 in range(0, d_ff, FFN_CHUNK):
        f1 = min(f0 + FFN_CHUNK, d_ff)
        gate = jnp.dot(h, wg_ref[:, f0:f1], preferred_element_type=F32)
        up = jnp.dot(h, wu_ref[:, f0:f1], preferred_element_type=F32)
        act = (gate * jax.nn.sigmoid(gate) * up).astype(BF16)
        part = jnp.dot(act, wd_ref[f0:f1, :], preferred_element_type=F32)
        ffn = part if ffn is None else ffn + part
    o_ref[...] = x1 + ffn


def _tail_call(x2, a, u, wpool_bd, pscale, w_out, g2, w_gate, w_up, w_down, seq):
    n = x2.shape[0]
    tq = TAIL_BLOCK
    d_ff = w_gate.shape[1]
    nblk = seq // tq
    halo_per_blk = tq // POOL_HALO
    n_halo = seq // POOL_HALO
    tok = lambda g: (g, 0)
    const = lambda g: (0, 0)
    prev_halo = lambda g: ((g // nblk) * n_halo + jnp.maximum((g % nblk) * halo_per_blk - 1, 0), 0)
    next_halo = lambda g: ((g // nblk) * n_halo
                           + jnp.minimum((g % nblk + 1) * halo_per_blk, n_halo - 1), 0)
    resident = pl.Buffered(1)
    kernel = functools.partial(_tail_kernel, seq=seq, blocks_per_seq=nblk)
    return pl.pallas_call(
        kernel,
        grid=(n // tq,),
        in_specs=[
            pl.BlockSpec((tq, D_MODEL), tok),
            pl.BlockSpec((tq, NA_WIDTH), tok),
            pl.BlockSpec((tq, POOL_WIDTH), tok),
            pl.BlockSpec((POOL_HALO, POOL_WIDTH), prev_halo),
            pl.BlockSpec((POOL_HALO, POOL_WIDTH), next_halo),
            pl.BlockSpec((POOL_WIDTH, POOL_WIDTH), const, pipeline_mode=resident),
            pl.BlockSpec((1, POOL_WIDTH), const),
            pl.BlockSpec((D_MODEL, D_MODEL), const, pipeline_mode=resident),
            pl.BlockSpec((1, D_MODEL), const),
            pl.BlockSpec((D_MODEL, d_ff), const, pipeline_mode=resident),
            pl.BlockSpec((D_MODEL, d_ff), const, pipeline_mode=resident),
            pl.BlockSpec((d_ff, D_MODEL), const, pipeline_mode=resident),
        ],
        out_specs=pl.BlockSpec((tq, D_MODEL), tok),
        out_shape=jax.ShapeDtypeStruct(x2.shape, F32),
        compiler_params=pltpu.CompilerParams(
            dimension_semantics=("arbitrary",), vmem_limit_bytes=VMEM_LIMIT_BYTES),
        name="tail",
    )(x2, a, u, u, u, wpool_bd, pscale, w_out, g2, w_gate, w_up, w_down)


def _bias_table(rpb):
    c = np.arange(GRID_W)
    col_start = np.clip(c - WIN_COLS // 2, 0, GRID_W - WIN_COLS)
    kc = np.arange(2 * GRID_W) % GRID_W
    inside = (kc[None, :] >= col_start[:, None]) & (kc[None, :] < col_start[:, None] + WIN_COLS)
    col_rel = kc[None, :] - c[:, None] + WIN_COLS - 1
    onehot = (col_rel[:, :, None] == np.arange(2 * WIN_COLS - 1)) & inside[:, :, None]
    both = jnp.einsum('hrj,cmj->hrcm', rpb.astype(F32), jnp.asarray(onehot, F32),
                      precision=lax.Precision.HIGHEST)
    low_lanes = np.arange(2 * GRID_W) < GRID_W
    tbl = jnp.where(low_lanes, both[:, :-1], both[:, 1:])
    return jnp.where(inside[None, None], tbl, MASK_VALUE)


def _block_diag(w_pool):
    g, d, _ = w_pool.shape
    eye = jnp.eye(g, dtype=w_pool.dtype)
    return jnp.einsum('gcd,gh->gchd', w_pool, eye).reshape(g * d, g * d)


def kernel(x, norm1_g, w_in, q_norm_g, k_norm_g, rpb, w_pool, pool_scale, w_out,
           norm2_g, w_gate, w_up, w_down):
    batch, seq, _ = x.shape
    depth = w_in.shape[0]
    x2 = x.reshape(batch * seq, D_MODEL)
    for l in range(depth):
        gq = jnp.tile(q_norm_g[l], NA_HEADS)[None, :]
        gk = jnp.tile(k_norm_g[l], NA_HEADS)[None, :]
        (q, k, v, u), (w_out_bf, w_gate_bf, w_up_bf, w_down_bf) = _proj_call(
            x2, norm1_g[l][None, :], w_in, l, gq, gk, (w_out, w_gate, w_up, w_down))
        a = _attn_call(q, k, v, _bias_table(rpb[l] * LOG2_E), batch, seq)
        x2 = _tail_call(x2, a, u, _block_diag(w_pool[l]).astype(BF16), pool_scale[l][None, :],
                        w_out_bf, norm2_g[l][None, :], w_gate_bf, w_up_bf, w_down_bf, seq)
    return x2.reshape(batch, seq, D_MODEL)
```

```python
import functools

import jax
import jax.numpy as jnp
import numpy as np
from jax import lax
from jax.experimental import pallas as pl
from jax.experimental.pallas import tpu as pltpu

D_MODEL = 1024
GRID_W = 64
NA_WIDTH = D_MODEL // 2
NA_HEADS = 8
NA_HEAD_DIM = NA_WIDTH // NA_HEADS
WIN_ROWS_MAX = 8
WIN_COLS = 16
POOL_WIDTH = D_MODEL - NA_WIDTH
POOL_WINDOWS = (2, 4, 8, 16)
POOL_GROUPS = len(POOL_WINDOWS)
POOL_GROUP_DIM = POOL_WIDTH // POOL_GROUPS
IN_WIDTH = 3 * NA_WIDTH + POOL_WIDTH
EPS = 1e-6

LANES = 128
HEAD_PAIRS = NA_WIDTH // LANES
MASK_VALUE = -1e30
LOG2_E = 1.4426950408889634
SCORE_SCALE = NA_HEAD_DIM ** -0.5 * LOG2_E
POOL_HALO = 16
VMEM_LIMIT_BYTES = 56 * 1024 * 1024

PROJ_BLOCK = 1024
N_LATER_WEIGHTS = 4
CAST_BANDS = 16
BF16_SUBLANES = 16
ATTN_ROWS_PER_STEP = 32
FFN_CHUNK = 1024
TAIL_BLOCK = 1024

F32 = jnp.float32
BF16 = jnp.bfloat16


def _rms_scale(x):
    return lax.rsqrt(jnp.mean(x * x, axis=-1, keepdims=True) + EPS)


def _proj_kernel(x_ref, g1_ref, w_ref, gq_ref, gk_ref, *refs):
    later_f32 = refs[:N_LATER_WEIGHTS]
    q_ref, k_ref, v_ref, u_ref = refs[N_LATER_WEIGHTS:N_LATER_WEIGHTS + 4]
    later_bf = refs[N_LATER_WEIGHTS + 4:2 * N_LATER_WEIGHTS + 4]
    w_bf = refs[-1]

    @pl.when(pl.program_id(0) == 0)
    def _():
        w_bf[...] = w_ref[...].astype(BF16)

    for src, dst in zip(later_f32, later_bf):
        dst[...] = src[...].astype(BF16)

    x = x_ref[...]
    h = (x * _rms_scale(x) * g1_ref[...]).astype(BF16)
    proj = jnp.dot(h, w_bf[...], preferred_element_type=F32)
    first_head = lax.broadcasted_iota(jnp.int32, (1, LANES), 1) < NA_HEAD_DIM

    def head_norm(t, g):
        tiles = []
        for p in range(HEAD_PAIRS):
            tp = t[:, p * LANES:(p + 1) * LANES]
            sq = tp * tp
            ss_first = jnp.sum(jnp.where(first_head, sq, 0.0), axis=-1, keepdims=True)
            ss_second = jnp.sum(jnp.where(first_head, 0.0, sq), axis=-1, keepdims=True)
            ms = jnp.where(first_head, ss_first, ss_second) * (1.0 / NA_HEAD_DIM)
            tiles.append(tp * lax.rsqrt(ms + EPS))
        return jnp.concatenate(tiles, axis=-1) * g

    q = head_norm(proj[:, :NA_WIDTH], gq_ref[...])
    k = head_norm(proj[:, NA_WIDTH:2 * NA_WIDTH], gk_ref[...])
    q = (q * SCORE_SCALE).astype(BF16)
    k = k.astype(BF16)
    v = proj[:, 2 * NA_WIDTH:3 * NA_WIDTH].astype(BF16)
    for p in range(HEAD_PAIRS):
        lanes = slice(p * LANES, (p + 1) * LANES)
        q_ref[p] = q[:, lanes]
        k_ref[p] = k[:, lanes]
        v_ref[p] = v[:, lanes]
    u_ref[...] = proj[:, 3 * NA_WIDTH:]


def _proj_call(x2, g1, w_in, layer, gq, gk, later_weights):
    assert len(later_weights) == N_LATER_WEIGHTS
    n = x2.shape[0]
    tm = PROJ_BLOCK
    steps = n // tm
    assert steps % CAST_BANDS == 0
    steps_per_band = steps // CAST_BANDS
    const = lambda i: (0, 0)
    row = lambda i: (i, 0)
    pair_row = lambda i: (0, i, 0)
    band_in, band_out, band_shape = [], [], []
    for w in later_weights:
        rows, cols = w.shape[1:]
        assert rows % (CAST_BANDS * BF16_SUBLANES) == 0
        band_in.append(pl.BlockSpec((None, rows // CAST_BANDS, cols),
                                    lambda i: (layer, i // steps_per_band, 0)))
        band_out.append(pl.BlockSpec((rows // CAST_BANDS, cols), lambda i: (i // steps_per_band, 0)))
        band_shape.append(jax.ShapeDtypeStruct((rows, cols), BF16))
    outs = pl.pallas_call(
        _proj_kernel,
        grid=(steps,),
        in_specs=[
            pl.BlockSpec((tm, D_MODEL), row),
            pl.BlockSpec((1, D_MODEL), const),
            pl.BlockSpec((None, D_MODEL, IN_WIDTH), lambda i: (layer, 0, 0),
                         pipeline_mode=pl.Buffered(1)),
            pl.BlockSpec((1, NA_WIDTH), const),
            pl.BlockSpec((1, NA_WIDTH), const),
        ] + band_in,
        out_specs=[
            pl.BlockSpec((HEAD_PAIRS, tm, LANES), pair_row),
            pl.BlockSpec((HEAD_PAIRS, tm, LANES), pair_row),
            pl.BlockSpec((HEAD_PAIRS, tm, LANES), pair_row),
            pl.BlockSpec((tm, POOL_WIDTH), row),
        ] + band_out,
        out_shape=[
            jax.ShapeDtypeStruct((HEAD_PAIRS, n, LANES), BF16),
            jax.ShapeDtypeStruct((HEAD_PAIRS, n, LANES), BF16),
            jax.ShapeDtypeStruct((HEAD_PAIRS, n, LANES), BF16),
            jax.ShapeDtypeStruct((n, POOL_WIDTH), F32),
        ] + band_shape,
        scratch_shapes=[pltpu.VMEM((D_MODEL, IN_WIDTH), BF16)],
        compiler_params=pltpu.CompilerParams(
            dimension_semantics=("arbitrary",), vmem_limit_bytes=VMEM_LIMIT_BYTES),
        name="proj",
    )(x2, g1, w_in, gq, gk, *later_weights)
    return outs[:4], outs[4:]


def _attn_kernel(q_ref, k_ref, v_ref, bias_ref, a_ref, s_scr, m_scr, p_scr, *, rows):
    wr = WIN_ROWS_MAX
    nkeys = wr * GRID_W
    lane = lax.broadcasted_iota(jnp.int32, (GRID_W, LANES), 1)
    first_head = lane < NA_HEAD_DIM

    def window_start(r):
        return jnp.clip(r - wr // 2, 0, rows - wr)

    def scores(r, slot):
        rs = window_start(r)
        first_rel = rs - r + WIN_ROWS_MAX - 1
        q0 = pl.multiple_of(r * GRID_W, GRID_W)
        k0 = pl.multiple_of(rs * GRID_W, GRID_W)
        for pair in range(HEAD_PAIRS):
            qs = q_ref[pair, pl.ds(q0, GRID_W), :]
            zero = jnp.zeros_like(qs)
            lhs = jnp.concatenate([jnp.where(first_head, qs, zero),
                                   jnp.where(first_head, zero, qs)], axis=0)
            kw = k_ref[pair, pl.ds(k0, nkeys), :]
            s = lax.dot_general(lhs, kw, (((1,), (1,)), ((), ())),
                                preferred_element_type=F32)
            s = s + jnp.concatenate(
                [jnp.concatenate([bias_ref[2 * pair + hh, first_rel + 2 * j] for hh in range(2)], axis=0)
                 for j in range(nkeys // LANES)], axis=1)
            s_scr[slot, pair] = s
            m_scr[slot, pair] = jnp.broadcast_to(jnp.max(s, axis=-1, keepdims=True),
                                                 (2 * GRID_W, LANES))

    def softmax(slot):
        for pair in range(HEAD_PAIRS):
            m = m_scr[slot, pair]
            for j in range(nkeys // LANES):
                p = jnp.exp2(s_scr[slot, pair, :, j * LANES:(j + 1) * LANES] - m)
                p_scr[slot, pair, :, j * LANES:(j + 1) * LANES] = p.astype(BF16)

    ones = jnp.ones((nkeys, LANES), BF16)

    def weighted_values(r, slot):
        q0 = pl.multiple_of(r * GRID_W, GRID_W)
        k0 = pl.multiple_of(window_start(r) * GRID_W, GRID_W)
        for pair in range(HEAD_PAIRS):
            vw = jnp.concatenate([v_ref[pair, pl.ds(k0, nkeys), :], ones], axis=1)
            ol = jnp.dot(p_scr[slot, pair], vw, preferred_element_type=F32)
            o = ol[:, :LANES] / ol[:, LANES:]
            oc = jnp.where(first_head, o[:GRID_W], o[GRID_W:])
            a_ref[pair, pl.ds(q0, GRID_W), :] = oc.astype(BF16)

    scores(0, 0)
    scores(1, 1)
    softmax(0)

    def row_group(j, carry):
        for k in range(ATTN_ROWS_PER_STEP):
            r = ATTN_ROWS_PER_STEP * j + k
            scores(jnp.minimum(r + 2, rows - 1), k % 2)
            softmax((k + 1) % 2)
            weighted_values(r, k % 2)
        return carry

    lax.fori_loop(0, rows // ATTN_ROWS_PER_STEP, row_group, 0)


def _attn_call(q, k, v, bias, batch, seq):
    rows = seq // GRID_W
    assert rows >= WIN_ROWS_MAX and rows % ATTN_ROWS_PER_STEP == 0
    nkeys = WIN_ROWS_MAX * GRID_W
    per_batch = lambda b: (0, b, 0)
    blk = pl.BlockSpec((HEAD_PAIRS, seq, LANES), per_batch)
    kernel = functools.partial(_attn_kernel, rows=rows)
    return pl.pallas_call(
        kernel,
        grid=(batch,),
        in_specs=[blk, blk, blk,
                  pl.BlockSpec(bias.shape, lambda b: (0, 0, 0, 0), pipeline_mode=pl.Buffered(1))],
        out_specs=blk,
        out_shape=jax.ShapeDtypeStruct(q.shape, BF16),
        scratch_shapes=[
            pltpu.VMEM((2, HEAD_PAIRS, 2 * GRID_W, nkeys), F32),
            pltpu.VMEM((2, HEAD_PAIRS, 2 * GRID_W, LANES), F32),
            pltpu.VMEM((2, HEAD_PAIRS, 2 * GRID_W, nkeys), BF16),
        ],
        compiler_params=pltpu.CompilerParams(
            dimension_semantics=("arbitrary",), vmem_limit_bytes=VMEM_LIMIT_BYTES),
        name="attn",
    )(q, k, v, bias)


def _pool_minus_token(ext, first_token, seq):
    n = ext.shape[0]
    half = POOL_HALO // 2
    tq = n - 2 * half

    def ahead(e, k):
        return pltpu.roll(e, n - k, axis=0)

    def window_count(t, w):
        return (jnp.minimum(t + w // 2, seq) - jnp.maximum(t - w // 2, 0)).astype(F32)

    edge = lax.broadcasted_iota(jnp.int32, (half, 1), 0)
    t_top = first_token + edge
    t_bot = first_token + tq - half + edge
    out = []
    for g, w in enumerate(POOL_WINDOWS):
        e = ext[:, g * POOL_GROUP_DIM:(g + 1) * POOL_GROUP_DIM]
        tok = e[half:half + tq]
        span = 1
        while 2 * span < w:
            e = e + ahead(e, span)
            span *= 2
        lo = half - w // 2
        first = ahead(e, lo)[:tq] if lo else e[:tq]
        s = first + e[half:half + tq]
        mean = jnp.concatenate([s[:half] / window_count(t_top, w),
                                s[half:tq - half] * (1.0 / w),
                                s[tq - half:] / window_count(t_bot, w)], axis=0)
        out.append(mean - tok)
    return jnp.concatenate(out, axis=-1)


def _tail_kernel(x_ref, a_ref, u_ref, up_ref, un_ref, wpool_ref, pscale_ref, wout_ref,
                 g2_ref, wg_ref, wu_ref, wd_ref, o_ref, *, seq, blocks_per_seq):
    i = pl.program_id(0) % blocks_per_seq
    tq = x_ref.shape[0]
    half = POOL_HALO // 2

    attn = jnp.concatenate([a_ref[p] for p in range(HEAD_PAIRS)], axis=-1)
    x1 = x_ref[...] + jnp.dot(attn, wout_ref[:NA_WIDTH, :], preferred_element_type=F32)

    prev_ok = (i > 0).astype(F32)
    next_ok = (i < blocks_per_seq - 1).astype(F32)
    ext = jnp.concatenate([up_ref[half:, :] * prev_ok, u_ref[...],
                           un_ref[:half, :] * next_ok], axis=0)
    d = _pool_minus_token(ext, i * tq, seq).astype(BF16)
    y = jnp.dot(d, wpool_ref[...], preferred_element_type=F32) * pscale_ref[...]
    x1 = x1 + jnp.dot(y.astype(BF16), wout_ref[NA_WIDTH:, :], preferred_element_type=F32)

    h = (x1 * _rms_scale(x1) * g2_ref[...]).astype(BF16)
    d_ff = wd_ref.shape[0]
    ffn = None
    for f0 in range(0, d_ff, FFN_CHUNK):
        f1 = min(f0 + FFN_CHUNK, d_ff)
        gate = jnp.dot(h, wg_ref[:, f0:f1], preferred_element_type=F32)
        up = jnp.dot(h, wu_ref[:, f0:f1], preferred_element_type=F32)
        act = (gate * jax.nn.sigmoid(gate) * up).astype(BF16)
        part = jnp.dot(act, wd_ref[f0:f1, :], preferred_element_type=F32)
        ffn = part if ffn is None else ffn + part
    o_ref[...] = x1 + ffn


def _tail_call(x2, a, u, wpool_bd, pscale, w_out, g2, w_gate, w_up, w_down, seq):
    n = x2.shape[0]
    tq = TAIL_BLOCK
    d_ff = w_gate.shape[1]
    nblk = seq // tq
    halo_per_blk = tq // POOL_HALO
    n_halo = seq // POOL_HALO
    tok = lambda g: (g, 0)
    const = lambda g: (0, 0)
    prev_halo = lambda g: ((g // nblk) * n_halo + jnp.maximum((g % nblk) * halo_per_blk - 1, 0), 0)
    next_halo = lambda g: ((g // nblk) * n_halo
                           + jnp.minimum((g % nblk + 1) * halo_per_blk, n_halo - 1), 0)
    resident = pl.Buffered(1)
    kernel = functools.partial(_tail_kernel, seq=seq, blocks_per_seq=nblk)
    return pl.pallas_call(
        kernel,
        grid=(n // tq,),
        in_specs=[
            pl.BlockSpec((tq, D_MODEL), tok),
            pl.BlockSpec((HEAD_PAIRS, tq, LANES), lambda g: (0, g, 0)),
            pl.BlockSpec((tq, POOL_WIDTH), tok),
            pl.BlockSpec((POOL_HALO, POOL_WIDTH), prev_halo),
            pl.BlockSpec((POOL_HALO, POOL_WIDTH), next_halo),
            pl.BlockSpec((POOL_WIDTH, POOL_WIDTH), const, pipeline_mode=resident),
            pl.BlockSpec((1, POOL_WIDTH), const),
            pl.BlockSpec((D_MODEL, D_MODEL), const, pipeline_mode=resident),
            pl.BlockSpec((1, D_MODEL), const),
            pl.BlockSpec((D_MODEL, d_ff), const, pipeline_mode=resident),
            pl.BlockSpec((D_MODEL, d_ff), const, pipeline_mode=resident),
            pl.BlockSpec((d_ff, D_MODEL), const, pipeline_mode=resident),
        ],
        out_specs=pl.BlockSpec((tq, D_MODEL), tok),
        out_shape=jax.ShapeDtypeStruct(x2.shape, F32),
        compiler_params=pltpu.CompilerParams(
            dimension_semantics=("arbitrary",), vmem_limit_bytes=VMEM_LIMIT_BYTES),
        name="tail",
    )(x2, a, u, u, u, wpool_bd, pscale, w_out, g2, w_gate, w_up, w_down)


def _bias_table(rpb):
    c = np.arange(GRID_W)
    col_start = np.clip(c - WIN_COLS // 2, 0, GRID_W - WIN_COLS)
    kc = np.arange(2 * GRID_W) % GRID_W
    inside = (kc[None, :] >= col_start[:, None]) & (kc[None, :] < col_start[:, None] + WIN_COLS)
    col_rel = kc[None, :] - c[:, None] + WIN_COLS - 1
    onehot = (col_rel[:, :, None] == np.arange(2 * WIN_COLS - 1)) & inside[:, :, None]
    both = jnp.einsum('hrj,cmj->hrcm', rpb.astype(F32), jnp.asarray(onehot, F32),
                      precision=lax.Precision.HIGHEST)
    low_lanes = np.arange(2 * GRID_W) < GRID_W
    tbl = jnp.where(low_lanes, both[:, :-1], both[:, 1:])
    return jnp.where(inside[None, None], tbl, MASK_VALUE)


def _block_diag(w_pool):
    g, d, _ = w_pool.shape
    eye = jnp.eye(g, dtype=w_pool.dtype)
    return jnp.einsum('gcd,gh->gchd', w_pool, eye).reshape(g * d, g * d)


def kernel(x, norm1_g, w_in, q_norm_g, k_norm_g, rpb, w_pool, pool_scale, w_out,
           norm2_g, w_gate, w_up, w_down):
    batch, seq, _ = x.shape
    depth = w_in.shape[0]
    x2 = x.reshape(batch * seq, D_MODEL)
    for l in range(depth):
        gq = jnp.tile(q_norm_g[l], NA_HEADS)[None, :]
        gk = jnp.tile(k_norm_g[l], NA_HEADS)[None, :]
        (q, k, v, u), (w_out_bf, w_gate_bf, w_up_bf, w_down_bf) = _proj_call(
            x2, norm1_g[l][None, :], w_in, l, gq, gk, (w_out, w_gate, w_up, w_down))
        a = _attn_call(q, k, v, _bias_table(rpb[l] * LOG2_E), batch, seq)
        x2 = _tail_call(x2, a, u, _block_diag(w_pool[l]).astype(BF16), pool_scale[l][None, :],
                        w_out_bf, norm2_g[l][None, :], w_gate_bf, w_up_bf, w_down_bf, seq)
    return x2.reshape(batch, seq, D_MODEL)
```

```python
import functools

import jax
import jax.numpy as jnp
import numpy as np
from jax import lax
from jax.experimental import pallas as pl
from jax.experimental.pallas import tpu as pltpu

D_MODEL = 1024
GRID_W = 64
NA_WIDTH = D_MODEL // 2
NA_HEADS = 8
NA_HEAD_DIM = NA_WIDTH // NA_HEADS
WIN_ROWS_MAX = 8
WIN_COLS = 16
POOL_WIDTH = D_MODEL - NA_WIDTH
POOL_WINDOWS = (2, 4, 8, 16)
POOL_GROUPS = len(POOL_WINDOWS)
POOL_GROUP_DIM = POOL_WIDTH // POOL_GROUPS
IN_WIDTH = 3 * NA_WIDTH + POOL_WIDTH
EPS = 1e-6

LANES = 128
HEAD_PAIRS = NA_WIDTH // LANES
MASK_VALUE = -1e30
LOG2_E = 1.4426950408889634
SCORE_SCALE = NA_HEAD_DIM ** -0.5 * LOG2_E
POOL_HALO = 16
VMEM_LIMIT_BYTES = 56 * 1024 * 1024

PROJ_BLOCK = 1024
PROJ_SLABS = 4
N_LATER_WEIGHTS = 4
CAST_BANDS = 16
BF16_SUBLANES = 16
ATTN_ROWS_PER_STEP = 32
FFN_CHUNK = 1024
TAIL_BLOCK = 1024

F32 = jnp.float32
BF16 = jnp.bfloat16


def _rms_scale(x):
    return lax.rsqrt(jnp.mean(x * x, axis=-1, keepdims=True) + EPS)


def _proj_kernel(x_ref, g1_ref, w_ref, gq_ref, gk_ref, *refs):
    later_f32 = refs[:N_LATER_WEIGHTS]
    q_ref, k_ref, v_ref, u_ref = refs[N_LATER_WEIGHTS:N_LATER_WEIGHTS + 4]
    later_bf = refs[N_LATER_WEIGHTS + 4:2 * N_LATER_WEIGHTS + 4]
    w_bf = refs[-1]

    @pl.when(pl.program_id(0) == 0)
    def _():
        w_bf[...] = w_ref[...].astype(BF16)

    first_head = lax.broadcasted_iota(jnp.int32, (1, LANES), 1) < NA_HEAD_DIM

    def head_norm(t, g):
        tiles = []
        for p in range(HEAD_PAIRS):
            tp = t[:, p * LANES:(p + 1) * LANES]
            sq = tp * tp
            ss_first = jnp.sum(jnp.where(first_head, sq, 0.0), axis=-1, keepdims=True)
            ss_second = jnp.sum(jnp.where(first_head, 0.0, sq), axis=-1, keepdims=True)
            ms = jnp.where(first_head, ss_first, ss_second) * (1.0 / NA_HEAD_DIM)
            tiles.append(tp * lax.rsqrt(ms + EPS))
        return jnp.concatenate(tiles, axis=-1) * g

    slab = x_ref.shape[0] // PROJ_SLABS
    for i in range(PROJ_SLABS):
        rows = slice(i * slab, (i + 1) * slab)
        x = x_ref[rows, :]
        h = (x * _rms_scale(x) * g1_ref[...]).astype(BF16)
        proj = jnp.dot(h, w_bf[...], preferred_element_type=F32)
        q = head_norm(proj[:, :NA_WIDTH], gq_ref[...])
        k = head_norm(proj[:, NA_WIDTH:2 * NA_WIDTH], gk_ref[...])
        q = (q * SCORE_SCALE).astype(BF16)
        k = k.astype(BF16)
        v = proj[:, 2 * NA_WIDTH:3 * NA_WIDTH].astype(BF16)
        for p in range(HEAD_PAIRS):
            lanes = slice(p * LANES, (p + 1) * LANES)
            q_ref[p, rows, :] = q[:, lanes]
            k_ref[p, rows, :] = k[:, lanes]
            v_ref[p, rows, :] = v[:, lanes]
        u_ref[rows, :] = proj[:, 3 * NA_WIDTH:]

    for src, dst in zip(later_f32, later_bf):
        dst[...] = src[...].astype(BF16)


def _proj_call(x2, g1, w_in, layer, gq, gk, later_weights):
    assert len(later_weights) == N_LATER_WEIGHTS
    n = x2.shape[0]
    tm = PROJ_BLOCK
    steps = n // tm
    assert steps % CAST_BANDS == 0
    steps_per_band = steps // CAST_BANDS
    const = lambda i: (0, 0)
    row = lambda i: (i, 0)
    pair_row = lambda i: (0, i, 0)
    band_in, band_out, band_shape = [], [], []
    for w in later_weights:
        rows, cols = w.shape[1:]
        assert rows % (CAST_BANDS * BF16_SUBLANES) == 0
        band_in.append(pl.BlockSpec((None, rows // CAST_BANDS, cols),
                                    lambda i: (layer, i // steps_per_band, 0)))
        band_out.append(pl.BlockSpec((rows // CAST_BANDS, cols), lambda i: (i // steps_per_band, 0)))
        band_shape.append(jax.ShapeDtypeStruct((rows, cols), BF16))
    outs = pl.pallas_call(
        _proj_kernel,
        grid=(steps,),
        in_specs=[
            pl.BlockSpec((tm, D_MODEL), row),
            pl.BlockSpec((1, D_MODEL), const),
            pl.BlockSpec((None, D_MODEL, IN_WIDTH), lambda i: (layer, 0, 0),
                         pipeline_mode=pl.Buffered(1)),
            pl.BlockSpec((1, NA_WIDTH), const),
            pl.BlockSpec((1, NA_WIDTH), const),
        ] + band_in,
        out_specs=[
            pl.BlockSpec((HEAD_PAIRS, tm, LANES), pair_row),
            pl.BlockSpec((HEAD_PAIRS, tm, LANES), pair_row),
            pl.BlockSpec((HEAD_PAIRS, tm, LANES), pair_row),
            pl.BlockSpec((tm, POOL_WIDTH), row),
        ] + band_out,
        out_shape=[
            jax.ShapeDtypeStruct((HEAD_PAIRS, n, LANES), BF16),
            jax.ShapeDtypeStruct((HEAD_PAIRS, n, LANES), BF16),
            jax.ShapeDtypeStruct((HEAD_PAIRS, n, LANES), BF16),
            jax.ShapeDtypeStruct((n, POOL_WIDTH), F32),
        ] + band_shape,
        scratch_shapes=[pltpu.VMEM((D_MODEL, IN_WIDTH), BF16)],
        compiler_params=pltpu.CompilerParams(
            dimension_semantics=("arbitrary",), vmem_limit_bytes=VMEM_LIMIT_BYTES),
        name="proj",
    )(x2, g1, w_in, gq, gk, *later_weights)
    return outs[:4], outs[4:]


def _attn_kernel(q_ref, k_ref, v_ref, bias_ref, a_ref, s_scr, m_scr, p_scr, *, rows):
    wr = WIN_ROWS_MAX
    nkeys = wr * GRID_W
    lane = lax.broadcasted_iota(jnp.int32, (GRID_W, LANES), 1)
    first_head = lane < NA_HEAD_DIM

    def window_start(r):
        return jnp.clip(r - wr // 2, 0, rows - wr)

    def scores(r, slot):
        rs = window_start(r)
        first_rel = rs - r + WIN_ROWS_MAX - 1
        q0 = pl.multiple_of(r * GRID_W, GRID_W)
        k0 = pl.multiple_of(rs * GRID_W, GRID_W)
        for pair in range(HEAD_PAIRS):
            qs = q_ref[pair, pl.ds(q0, GRID_W), :]
            zero = jnp.zeros_like(qs)
            lhs = jnp.concatenate([jnp.where(first_head, qs, zero),
                                   jnp.where(first_head, zero, qs)], axis=0)
            kw = k_ref[pair, pl.ds(k0, nkeys), :]
            s = lax.dot_general(lhs, kw, (((1,), (1,)), ((), ())),
                                preferred_element_type=F32)
            s = s + jnp.concatenate(
                [jnp.concatenate([bias_ref[2 * pair + hh, first_rel + 2 * j] for hh in range(2)], axis=0)
                 for j in range(nkeys // LANES)], axis=1)
            s_scr[slot, pair] = s
            m_scr[slot, pair] = jnp.broadcast_to(jnp.max(s, axis=-1, keepdims=True),
                                                 (2 * GRID_W, LANES))

    def softmax(slot):
        for pair in range(HEAD_PAIRS):
            m = m_scr[slot, pair]
            for j in range(nkeys // LANES):
                p = jnp.exp2(s_scr[slot, pair, :, j * LANES:(j + 1) * LANES] - m)
                p_scr[slot, pair, :, j * LANES:(j + 1) * LANES] = p.astype(BF16)

    ones = jnp.ones((nkeys, LANES), BF16)

    def weighted_values(r, slot):
        q0 = pl.multiple_of(r * GRID_W, GRID_W)
        k0 = pl.multiple_of(window_start(r) * GRID_W, GRID_W)
        for pair in range(HEAD_PAIRS):
            vw = jnp.concatenate([v_ref[pair, pl.ds(k0, nkeys), :], ones], axis=1)
            ol = jnp.dot(p_scr[slot, pair], vw, preferred_element_type=F32)
            o = ol[:, :LANES] / ol[:, LANES:]
            oc = jnp.where(first_head, o[:GRID_W], o[GRID_W:])
            a_ref[pair, pl.ds(q0, GRID_W), :] = oc.astype(BF16)

    scores(0, 0)
    scores(1, 1)
    softmax(0)

    def row_group(j, carry):
        for k in range(ATTN_ROWS_PER_STEP):
            r = ATTN_ROWS_PER_STEP * j + k
            scores(jnp.minimum(r + 2, rows - 1), k % 2)
            softmax((k + 1) % 2)
            weighted_values(r, k % 2)
        return carry

    lax.fori_loop(0, rows // ATTN_ROWS_PER_STEP, row_group, 0)


def _attn_call(q, k, v, bias, batch, seq):
    rows = seq // GRID_W
    assert rows >= WIN_ROWS_MAX and rows % ATTN_ROWS_PER_STEP == 0
    nkeys = WIN_ROWS_MAX * GRID_W
    per_batch = lambda b: (0, b, 0)
    blk = pl.BlockSpec((HEAD_PAIRS, seq, LANES), per_batch)
    kernel = functools.partial(_attn_kernel, rows=rows)
    return pl.pallas_call(
        kernel,
        grid=(batch,),
        in_specs=[blk, blk, blk,
                  pl.BlockSpec(bias.shape, lambda b: (0, 0, 0, 0), pipeline_mode=pl.Buffered(1))],
        out_specs=blk,
        out_shape=jax.ShapeDtypeStruct(q.shape, BF16),
        scratch_shapes=[
            pltpu.VMEM((2, HEAD_PAIRS, 2 * GRID_W, nkeys), F32),
            pltpu.VMEM((2, HEAD_PAIRS, 2 * GRID_W, LANES), F32),
            pltpu.VMEM((2, HEAD_PAIRS, 2 * GRID_W, nkeys), BF16),
        ],
        compiler_params=pltpu.CompilerParams(
            dimension_semantics=("arbitrary",), vmem_limit_bytes=VMEM_LIMIT_BYTES),
        name="attn",
    )(q, k, v, bias)


def _pool_minus_token(ext, first_token, seq):
    n = ext.shape[0]
    half = POOL_HALO // 2
    tq = n - 2 * half

    def ahead(e, k):
        return pltpu.roll(e, n - k, axis=0)

    def window_count(t, w):
        return (jnp.minimum(t + w // 2, seq) - jnp.maximum(t - w // 2, 0)).astype(F32)

    edge = lax.broadcasted_iota(jnp.int32, (half, 1), 0)
    t_top = first_token + edge
    t_bot = first_token + tq - half + edge
    out = []
    for g, w in enumerate(POOL_WINDOWS):
        e = ext[:, g * POOL_GROUP_DIM:(g + 1) * POOL_GROUP_DIM]
        tok = e[half:half + tq]
        span = 1
        while 2 * span < w:
            e = e + ahead(e, span)
            span *= 2
        lo = half - w // 2
        first = ahead(e, lo)[:tq] if lo else e[:tq]
        s = first + e[half:half + tq]
        mean = jnp.concatenate([s[:half] / window_count(t_top, w),
                                s[half:tq - half] * (1.0 / w),
                                s[tq - half:] / window_count(t_bot, w)], axis=0)
        out.append(mean - tok)
    return jnp.concatenate(out, axis=-1)


def _tail_kernel(x_ref, a_ref, u_ref, up_ref, un_ref, wpool_ref, pscale_ref, wout_ref,
                 g2_ref, wg_ref, wu_ref, wd_ref, o_ref, *, seq, blocks_per_seq):
    i = pl.program_id(0) % blocks_per_seq
    tq = x_ref.shape[0]
    half = POOL_HALO // 2

    attn = jnp.concatenate([a_ref[p] for p in range(HEAD_PAIRS)], axis=-1)
    x1 = x_ref[...] + jnp.dot(attn, wout_ref[:NA_WIDTH, :], preferred_element_type=F32)

    prev_ok = (i > 0).astype(F32)
    next_ok = (i < blocks_per_seq - 1).astype(F32)
    ext = jnp.concatenate([up_ref[half:, :] * prev_ok, u_ref[...],
                           un_ref[:half, :] * next_ok], axis=0)
    d = _pool_minus_token(ext, i * tq, seq).astype(BF16)
    y = jnp.dot(d, wpool_ref[...], preferred_element_type=F32) * pscale_ref[...]
    x1 = x1 + jnp.dot(y.astype(BF16), wout_ref[NA_WIDTH:, :], preferred_element_type=F32)

    h = (x1 * _rms_scale(x1) * g2_ref[...]).astype(BF16)
    d_ff = wd_ref.shape[0]
    ffn = None
    for f0 in range(0, d_ff, FFN_CHUNK):
        f1 = min(f0 + FFN_CHUNK, d_ff)
        gate = jnp.dot(h, wg_ref[:, f0:f1], preferred_element_type=F32)
        up = jnp.dot(h, wu_ref[:, f0:f1], preferred_element_type=F32)
        act = (gate * jax.nn.sigmoid(gate) * up).astype(BF16)
        part = jnp.dot(act, wd_ref[f0:f1, :], preferred_element_type=F32)
        ffn = part if ffn is None else ffn + part
    o_ref[...] = x1 + ffn


def _tail_call(x2, a, u, wpool_bd, pscale, w_out, g2, w_gate, w_up, w_down, seq):
    n = x2.shape[0]
    tq = TAIL_BLOCK
    d_ff = w_gate.shape[1]
    nblk = seq // tq
    halo_per_blk = tq // POOL_HALO
    n_halo = seq // POOL_HALO
    tok = lambda g: (g, 0)
    const = lambda g: (0, 0)
    prev_halo = lambda g: ((g // nblk) * n_halo + jnp.maximum((g % nblk) * halo_per_blk - 1, 0), 0)
    next_halo = lambda g: ((g // nblk) * n_halo
                           + jnp.minimum((g % nblk + 1) * halo_per_blk, n_halo - 1), 0)
    resident = pl.Buffered(1)
    kernel = functools.partial(_tail_kernel, seq=seq, blocks_per_seq=nblk)
    return pl.pallas_call(
        kernel,
        grid=(n // tq,),
        in_specs=[
            pl.BlockSpec((tq, D_MODEL), tok),
            pl.BlockSpec((HEAD_PAIRS, tq, LANES), lambda g: (0, g, 0)),
            pl.BlockSpec((tq, POOL_WIDTH), tok),
            pl.BlockSpec((POOL_HALO, POOL_WIDTH), prev_halo),
            pl.BlockSpec((POOL_HALO, POOL_WIDTH), next_halo),
            pl.BlockSpec((POOL_WIDTH, POOL_WIDTH), const, pipeline_mode=resident),
            pl.BlockSpec((1, POOL_WIDTH), const),
            pl.BlockSpec((D_MODEL, D_MODEL), const, pipeline_mode=resident),
            pl.BlockSpec((1, D_MODEL), const),
            pl.BlockSpec((D_MODEL, d_ff), const, pipeline_mode=resident),
            pl.BlockSpec((D_MODEL, d_ff), const, pipeline_mode=resident),
            pl.BlockSpec((d_ff, D_MODEL), const, pipeline_mode=resident),
        ],
        out_specs=pl.BlockSpec((tq, D_MODEL), tok),
        out_shape=jax.ShapeDtypeStruct(x2.shape, F32),
        compiler_params=pltpu.CompilerParams(
            dimension_semantics=("arbitrary",), vmem_limit_bytes=VMEM_LIMIT_BYTES),
        name="tail",
    )(x2, a, u, u, u, wpool_bd, pscale, w_out, g2, w_gate, w_up, w_down)


def _bias_table(rpb):
    c = np.arange(GRID_W)
    col_start = np.clip(c - WIN_COLS // 2, 0, GRID_W - WIN_COLS)
    kc = np.arange(2 * GRID_W) % GRID_W
    inside = (kc[None, :] >= col_start[:, None]) & (kc[None, :] < col_start[:, None] + WIN_COLS)
    col_rel = kc[None, :] - c[:, None] + WIN_COLS - 1
    onehot = (col_rel[:, :, None] == np.arange(2 * WIN_COLS - 1)) & inside[:, :, None]
    both = jnp.einsum('hrj,cmj->hrcm', rpb.astype(F32), jnp.asarray(onehot, F32),
                      precision=lax.Precision.HIGHEST)
    low_lanes = np.arange(2 * GRID_W) < GRID_W
    tbl = jnp.where(low_lanes, both[:, :-1], both[:, 1:])
    return jnp.where(inside[None, None], tbl, MASK_VALUE)


def _block_diag(w_pool):
    g, d, _ = w_pool.shape
    eye = jnp.eye(g, dtype=w_pool.dtype)
    return jnp.einsum('gcd,gh->gchd', w_pool, eye).reshape(g * d, g * d)


def kernel(x, norm1_g, w_in, q_norm_g, k_norm_g, rpb, w_pool, pool_scale, w_out,
           norm2_g, w_gate, w_up, w_down):
    batch, seq, _ = x.shape
    depth = w_in.shape[0]
    x2 = x.reshape(batch * seq, D_MODEL)
    for l in range(depth):
        gq = jnp.tile(q_norm_g[l], NA_HEADS)[None, :]
        gk = jnp.tile(k_norm_g[l], NA_HEADS)[None, :]
        (q, k, v, u), (w_out_bf, w_gate_bf, w_up_bf, w_down_bf) = _proj_call(
            x2, norm1_g[l][None, :], w_in, l, gq, gk, (w_out, w_gate, w_up, w_down))
        a = _attn_call(q, k, v, _bias_table(rpb[l] * LOG2_E), batch, seq)
        x2 = _tail_call(x2, a, u, _block_diag(w_pool[l]).astype(BF16), pool_scale[l][None, :],
                        w_out_bf, norm2_g[l][None, :], w_gate_bf, w_up_bf, w_down_bf, seq)
    return x2.reshape(batch, seq, D_MODEL)
```

```python
import functools

import jax
import jax.numpy as jnp
import numpy as np
from jax import lax
from jax.experimental import pallas as pl
from jax.experimental.pallas import tpu as pltpu

D_MODEL = 1024
GRID_W = 64
NA_WIDTH = D_MODEL // 2
NA_HEADS = 8
NA_HEAD_DIM = NA_WIDTH // NA_HEADS
WIN_ROWS_MAX = 8
WIN_COLS = 16
POOL_WIDTH = D_MODEL - NA_WIDTH
POOL_WINDOWS = (2, 4, 8, 16)
POOL_GROUPS = len(POOL_WINDOWS)
POOL_GROUP_DIM = POOL_WIDTH // POOL_GROUPS
IN_WIDTH = 3 * NA_WIDTH + POOL_WIDTH
EPS = 1e-6

LANES = 128
HEAD_PAIRS = NA_WIDTH // LANES
MASK_VALUE = -1e30
LOG2_E = 1.4426950408889634
SCORE_SCALE = NA_HEAD_DIM ** -0.5 * LOG2_E
POOL_HALO = 16
VMEM_LIMIT_BYTES = 56 * 1024 * 1024

PROJ_BLOCK = 1024
N_LATER_WEIGHTS = 4
CAST_BANDS = 16
BF16_SUBLANES = 16
ATTN_ROWS_PER_STEP = 32
FFN_CHUNK = 1024
TAIL_BLOCK = 1024

F32 = jnp.float32
BF16 = jnp.bfloat16


def _rms_scale(x):
    return lax.rsqrt(jnp.mean(x * x, axis=-1, keepdims=True) + EPS)


def _proj_kernel(x_ref, g1_ref, w_ref, gq_ref, gk_ref, *refs):
    later_f32 = refs[:N_LATER_WEIGHTS]
    q_ref, k_ref, v_ref, u_ref = refs[N_LATER_WEIGHTS:N_LATER_WEIGHTS + 4]
    later_bf = refs[N_LATER_WEIGHTS + 4:2 * N_LATER_WEIGHTS + 4]
    w_bf = refs[-1]

    @pl.when(pl.program_id(0) == 0)
    def _():
        w_bf[...] = w_ref[...].astype(BF16)

    for src, dst in zip(later_f32, later_bf):
        dst[...] = src[...].astype(BF16)

    x = x_ref[...]
    h = (x * _rms_scale(x) * g1_ref[...]).astype(BF16)
    proj = jnp.dot(h, w_bf[...], preferred_element_type=F32)
    first_head = lax.broadcasted_iota(jnp.int32, (1, LANES), 1) < NA_HEAD_DIM

    def head_norm(t, g):
        tiles = []
        for p in range(HEAD_PAIRS):
            tp = t[:, p * LANES:(p + 1) * LANES]
            sq = tp * tp
            ss_first = jnp.sum(jnp.where(first_head, sq, 0.0), axis=-1, keepdims=True)
            ss_second = jnp.sum(jnp.where(first_head, 0.0, sq), axis=-1, keepdims=True)
            ms = jnp.where(first_head, ss_first, ss_second) * (1.0 / NA_HEAD_DIM)
            tiles.append(tp * lax.rsqrt(ms + EPS))
        return jnp.concatenate(tiles, axis=-1) * g

    q = head_norm(proj[:, :NA_WIDTH], gq_ref[...])
    k = head_norm(proj[:, NA_WIDTH:2 * NA_WIDTH], gk_ref[...])
    q = (q * SCORE_SCALE).astype(BF16)
    k = k.astype(BF16)
    v = proj[:, 2 * NA_WIDTH:3 * NA_WIDTH].astype(BF16)
    for p in range(HEAD_PAIRS):
        lanes = slice(p * LANES, (p + 1) * LANES)
        q_ref[p] = q[:, lanes]
        k_ref[p] = k[:, lanes]
        v_ref[p] = v[:, lanes]
    u_ref[...] = proj[:, 3 * NA_WIDTH:]


def _proj_call(x2, g1, w_in, layer, gq, gk, later_weights):
    assert len(later_weights) == N_LATER_WEIGHTS
    n = x2.shape[0]
    tm = PROJ_BLOCK
    steps = n // tm
    assert steps % CAST_BANDS == 0
    steps_per_band = steps // CAST_BANDS
    const = lambda i: (0, 0)
    row = lambda i: (i, 0)
    pair_row = lambda i: (0, i, 0)
    band_in, band_out, band_shape = [], [], []
    for w in later_weights:
        rows, cols = w.shape[1:]
        assert rows % (CAST_BANDS * BF16_SUBLANES) == 0
        band_in.append(pl.BlockSpec((None, rows // CAST_BANDS, cols),
                                    lambda i: (layer, i // steps_per_band, 0)))
        band_out.append(pl.BlockSpec((rows // CAST_BANDS, cols), lambda i: (i // steps_per_band, 0)))
        band_shape.append(jax.ShapeDtypeStruct((rows, cols), BF16))
    outs = pl.pallas_call(
        _proj_kernel,
        grid=(steps,),
        in_specs=[
            pl.BlockSpec((tm, D_MODEL), row),
            pl.BlockSpec((1, D_MODEL), const),
            pl.BlockSpec((None, D_MODEL, IN_WIDTH), lambda i: (layer, 0, 0),
                         pipeline_mode=pl.Buffered(1)),
            pl.BlockSpec((1, NA_WIDTH), const),
            pl.BlockSpec((1, NA_WIDTH), const),
        ] + band_in,
        out_specs=[
            pl.BlockSpec((HEAD_PAIRS, tm, LANES), pair_row),
            pl.BlockSpec((HEAD_PAIRS, tm, LANES), pair_row),
            pl.BlockSpec((HEAD_PAIRS, tm, LANES), pair_row),
            pl.BlockSpec((tm, POOL_WIDTH), row),
        ] + band_out,
        out_shape=[
            jax.ShapeDtypeStruct((HEAD_PAIRS, n, LANES), BF16),
            jax.ShapeDtypeStruct((HEAD_PAIRS, n, LANES), BF16),
            jax.ShapeDtypeStruct((HEAD_PAIRS, n, LANES), BF16),
            jax.ShapeDtypeStruct((n, POOL_WIDTH), F32),
        ] + band_shape,
        scratch_shapes=[pltpu.VMEM((D_MODEL, IN_WIDTH), BF16)],
        compiler_params=pltpu.CompilerParams(
            dimension_semantics=("arbitrary",), vmem_limit_bytes=VMEM_LIMIT_BYTES),
        name="proj",
    )(x2, g1, w_in, gq, gk, *later_weights)
    return outs[:4], outs[4:]


def _attn_kernel(q_ref, k_ref, v_ref, bias_ref, a_ref, s_scr, m_scr, p_scr, *, rows):
    wr = WIN_ROWS_MAX
    nkeys = wr * GRID_W
    lane = lax.broadcasted_iota(jnp.int32, (GRID_W, LANES), 1)
    first_head = lane < NA_HEAD_DIM

    def window_start(r):
        return jnp.clip(r - wr // 2, 0, rows - wr)

    def scores(r, slot):
        rs = window_start(r)
        first_rel = rs - r + WIN_ROWS_MAX - 1
        q0 = pl.multiple_of(r * GRID_W, GRID_W)
        k0 = pl.multiple_of(rs * GRID_W, GRID_W)
        for pair in range(HEAD_PAIRS):
            qs = q_ref[pair, pl.ds(q0, GRID_W), :]
            zero = jnp.zeros_like(qs)
            lhs = jnp.concatenate([jnp.where(first_head, qs, zero),
                                   jnp.where(first_head, zero, qs)], axis=0)
            kw = k_ref[pair, pl.ds(k0, nkeys), :]
            s = lax.dot_general(lhs, kw, (((1,), (1,)), ((), ())),
                                preferred_element_type=F32)
            s = s + jnp.concatenate(
                [jnp.concatenate([bias_ref[2 * pair + hh, first_rel + 2 * j] for hh in range(2)], axis=0)
                 for j in range(nkeys // LANES)], axis=1)
            s_scr[slot, pair] = s
            m_scr[slot, pair] = jnp.broadcast_to(jnp.max(s, axis=-1, keepdims=True),
                                                 (2 * GRID_W, LANES))

    def softmax(slot):
        for pair in range(HEAD_PAIRS):
            m = m_scr[slot, pair]
            for j in range(nkeys // LANES):
                p = jnp.exp2(s_scr[slot, pair, :, j * LANES:(j + 1) * LANES] - m)
                p_scr[slot, pair, :, j * LANES:(j + 1) * LANES] = p.astype(BF16)

    ones = jnp.ones((nkeys, LANES), BF16)

    def weighted_values(r, slot):
        q0 = pl.multiple_of(r * GRID_W, GRID_W)
        k0 = pl.multiple_of(window_start(r) * GRID_W, GRID_W)
        for pair in range(HEAD_PAIRS):
            vw = jnp.concatenate([v_ref[pair, pl.ds(k0, nkeys), :], ones], axis=1)
            ol = jnp.dot(p_scr[slot, pair], vw, preferred_element_type=F32)
            merged = jnp.where(jnp.concatenate([first_head, first_head], axis=1),
                               ol[:GRID_W], ol[GRID_W:])
            a_ref[pair, pl.ds(q0, GRID_W), :] = (merged[:, :LANES] / merged[:, LANES:]).astype(BF16)

    scores(0, 0)
    scores(1, 1)
    softmax(0)

    def row_group(j, carry):
        for k in range(ATTN_ROWS_PER_STEP):
            r = ATTN_ROWS_PER_STEP * j + k
            scores(jnp.minimum(r + 2, rows - 1), k % 2)
            softmax((k + 1) % 2)
            weighted_values(r, k % 2)
        return carry

    lax.fori_loop(0, rows // ATTN_ROWS_PER_STEP, row_group, 0)


def _attn_call(q, k, v, bias, batch, seq):
    rows = seq // GRID_W
    assert rows >= WIN_ROWS_MAX and rows % ATTN_ROWS_PER_STEP == 0
    nkeys = WIN_ROWS_MAX * GRID_W
    per_batch = lambda b: (0, b, 0)
    blk = pl.BlockSpec((HEAD_PAIRS, seq, LANES), per_batch)
    kernel = functools.partial(_attn_kernel, rows=rows)
    return pl.pallas_call(
        kernel,
        grid=(batch,),
        in_specs=[blk, blk, blk,
                  pl.BlockSpec(bias.shape, lambda b: (0, 0, 0, 0), pipeline_mode=pl.Buffered(1))],
        out_specs=blk,
        out_shape=jax.ShapeDtypeStruct(q.shape, BF16),
        scratch_shapes=[
            pltpu.VMEM((2, HEAD_PAIRS, 2 * GRID_W, nkeys), F32),
            pltpu.VMEM((2, HEAD_PAIRS, 2 * GRID_W, LANES), F32),
            pltpu.VMEM((2, HEAD_PAIRS, 2 * GRID_W, nkeys), BF16),
        ],
        compiler_params=pltpu.CompilerParams(
            dimension_semantics=("arbitrary",), vmem_limit_bytes=VMEM_LIMIT_BYTES),
        name="attn",
    )(q, k, v, bias)


def _pool_minus_token(ext, first_token, seq):
    n = ext.shape[0]
    half = POOL_HALO // 2
    tq = n - 2 * half

    def ahead(e, k):
        return pltpu.roll(e, n - k, axis=0)

    def window_count(t, w):
        return (jnp.minimum(t + w // 2, seq) - jnp.maximum(t - w // 2, 0)).astype(F32)

    edge = lax.broadcasted_iota(jnp.int32, (half, 1), 0)
    t_top = first_token + edge
    t_bot = first_token + tq - half + edge
    out = []
    for g, w in enumerate(POOL_WINDOWS):
        e = ext[:, g * POOL_GROUP_DIM:(g + 1) * POOL_GROUP_DIM]
        tok = e[half:half + tq]
        span = 1
        while 2 * span < w:
            e = e + ahead(e, span)
            span *= 2
        lo = half - w // 2
        first = ahead(e, lo)[:tq] if lo else e[:tq]
        s = first + e[half:half + tq]
        mean = jnp.concatenate([s[:half] / window_count(t_top, w),
                                s[half:tq - half] * (1.0 / w),
                                s[tq - half:] / window_count(t_bot, w)], axis=0)
        out.append(mean - tok)
    return jnp.concatenate(out, axis=-1)


def _tail_kernel(x_ref, a_ref, u_ref, up_ref, un_ref, wpool_ref, pscale_ref, wout_ref,
                 g2_ref, wg_ref, wu_ref, wd_ref, o_ref, *, seq, blocks_per_seq):
    i = pl.program_id(0) % blocks_per_seq
    tq = x_ref.shape[0]
    half = POOL_HALO // 2

    attn = jnp.concatenate([a_ref[p] for p in range(HEAD_PAIRS)], axis=-1)
    x1 = x_ref[...] + jnp.dot(attn, wout_ref[:NA_WIDTH, :], preferred_element_type=F32)

    prev_ok = (i > 0).astype(F32)
    next_ok = (i < blocks_per_seq - 1).astype(F32)
    ext = jnp.concatenate([up_ref[half:, :] * prev_ok, u_ref[...],
                           un_ref[:half, :] * next_ok], axis=0)
    d = _pool_minus_token(ext, i * tq, seq).astype(BF16)
    y = jnp.dot(d, wpool_ref[...], preferred_element_type=F32) * pscale_ref[...]
    x1 = x1 + jnp.dot(y.astype(BF16), wout_ref[NA_WIDTH:, :], preferred_element_type=F32)

    h = (x1 * _rms_scale(x1) * g2_ref[...]).astype(BF16)
    d_ff = wd_ref.shape[0]
    ffn = None
    for f0 in range(0, d_ff, FFN_CHUNK):
        f1 = min(f0 + FFN_CHUNK, d_ff)
        gate = jnp.dot(h, wg_ref[:, f0:f1], preferred_element_type=F32)
        up = jnp.dot(h, wu_ref[:, f0:f1], preferred_element_type=F32)
        act = (gate * jax.nn.sigmoid(gate) * up).astype(BF16)
        part = jnp.dot(act, wd_ref[f0:f1, :], preferred_element_type=F32)
        ffn = part if ffn is None else ffn + part
    o_ref[...] = x1 + ffn


def _tail_call(x2, a, u, wpool_bd, pscale, w_out, g2, w_gate, w_up, w_down, seq):
    n = x2.shape[0]
    tq = TAIL_BLOCK
    d_ff = w_gate.shape[1]
    nblk = seq // tq
    halo_per_blk = tq // POOL_HALO
    n_halo = seq // POOL_HALO
    tok = lambda g: (g, 0)
    const = lambda g: (0, 0)
    prev_halo = lambda g: ((g // nblk) * n_halo + jnp.maximum((g % nblk) * halo_per_blk - 1, 0), 0)
    next_halo = lambda g: ((g // nblk) * n_halo
                           + jnp.minimum((g % nblk + 1) * halo_per_blk, n_halo - 1), 0)
    resident = pl.Buffered(1)
    kernel = functools.partial(_tail_kernel, seq=seq, blocks_per_seq=nblk)
    return pl.pallas_call(
        kernel,
        grid=(n // tq,),
        in_specs=[
            pl.BlockSpec((tq, D_MODEL), tok),
            pl.BlockSpec((HEAD_PAIRS, tq, LANES), lambda g: (0, g, 0)),
            pl.BlockSpec((tq, POOL_WIDTH), tok),
            pl.BlockSpec((POOL_HALO, POOL_WIDTH), prev_halo),
            pl.BlockSpec((POOL_HALO, POOL_WIDTH), next_halo),
            pl.BlockSpec((POOL_WIDTH, POOL_WIDTH), const, pipeline_mode=resident),
            pl.BlockSpec((1, POOL_WIDTH), const),
            pl.BlockSpec((D_MODEL, D_MODEL), const, pipeline_mode=resident),
            pl.BlockSpec((1, D_MODEL), const),
            pl.BlockSpec((D_MODEL, d_ff), const, pipeline_mode=resident),
            pl.BlockSpec((D_MODEL, d_ff), const, pipeline_mode=resident),
            pl.BlockSpec((d_ff, D_MODEL), const, pipeline_mode=resident),
        ],
        out_specs=pl.BlockSpec((tq, D_MODEL), tok),
        out_shape=jax.ShapeDtypeStruct(x2.shape, F32),
        compiler_params=pltpu.CompilerParams(
            dimension_semantics=("arbitrary",), vmem_limit_bytes=VMEM_LIMIT_BYTES),
        name="tail",
    )(x2, a, u, u, u, wpool_bd, pscale, w_out, g2, w_gate, w_up, w_down)


def _bias_table(rpb):
    c = np.arange(GRID_W)
    col_start = np.clip(c - WIN_COLS // 2, 0, GRID_W - WIN_COLS)
    kc = np.arange(2 * GRID_W) % GRID_W
    inside = (kc[None, :] >= col_start[:, None]) & (kc[None, :] < col_start[:, None] + WIN_COLS)
    col_rel = kc[None, :] - c[:, None] + WIN_COLS - 1
    onehot = (col_rel[:, :, None] == np.arange(2 * WIN_COLS - 1)) & inside[:, :, None]
    both = jnp.einsum('hrj,cmj->hrcm', rpb.astype(F32), jnp.asarray(onehot, F32),
                      precision=lax.Precision.HIGHEST)
    low_lanes = np.arange(2 * GRID_W) < GRID_W
    tbl = jnp.where(low_lanes, both[:, :-1], both[:, 1:])
    return jnp.where(inside[None, None], tbl, MASK_VALUE)


def _block_diag(w_pool):
    g, d, _ = w_pool.shape
    eye = jnp.eye(g, dtype=w_pool.dtype)
    return jnp.einsum('gcd,gh->gchd', w_pool, eye).reshape(g * d, g * d)


def kernel(x, norm1_g, w_in, q_norm_g, k_norm_g, rpb, w_pool, pool_scale, w_out,
           norm2_g, w_gate, w_up, w_down):
    batch, seq, _ = x.shape
    depth = w_in.shape[0]
    x2 = x.reshape(batch * seq, D_MODEL)
    for l in range(depth):
        gq = jnp.tile(q_norm_g[l], NA_HEADS)[None, :]
        gk = jnp.tile(k_norm_g[l], NA_HEADS)[None, :]
        (q, k, v, u), (w_out_bf, w_gate_bf, w_up_bf, w_down_bf) = _proj_call(
            x2, norm1_g[l][None, :], w_in, l, gq, gk, (w_out, w_gate, w_up, w_down))
        a = _attn_call(q, k, v, _bias_table(rpb[l] * LOG2_E), batch, seq)
        x2 = _tail_call(x2, a, u, _block_diag(w_pool[l]).astype(BF16), pool_scale[l][None, :],
                        w_out_bf, norm2_g[l][None, :], w_gate_bf, w_up_bf, w_down_bf, seq)
    return x2.reshape(batch, seq, D_MODEL)
```

```python
import functools

import jax
import jax.numpy as jnp
import numpy as np
from jax import lax
from jax.experimental import pallas as pl
from jax.experimental.pallas import tpu as pltpu

D_MODEL = 1024
GRID_W = 64
NA_WIDTH = D_MODEL // 2
NA_HEADS = 8
NA_HEAD_DIM = NA_WIDTH // NA_HEADS
WIN_ROWS_MAX = 8
WIN_COLS = 16
POOL_WIDTH = D_MODEL - NA_WIDTH
POOL_WINDOWS = (2, 4, 8, 16)
POOL_GROUPS = len(POOL_WINDOWS)
POOL_GROUP_DIM = POOL_WIDTH // POOL_GROUPS
IN_WIDTH = 3 * NA_WIDTH + POOL_WIDTH
EPS = 1e-6

LANES = 128
HEAD_PAIRS = NA_WIDTH // LANES
MASK_VALUE = -1e30
LOG2_E = 1.4426950408889634
SCORE_SCALE = NA_HEAD_DIM ** -0.5 * LOG2_E
POOL_HALO = 16
VMEM_LIMIT_BYTES = 56 * 1024 * 1024

PROJ_BLOCK = 1024
N_LATER_WEIGHTS = 4
CAST_BANDS = 16
BF16_SUBLANES = 16
ATTN_ROWS_PER_STEP = 32
FFN_CHUNK = 1024
TAIL_BLOCK = 1024

F32 = jnp.float32
BF16 = jnp.bfloat16


def _rms_scale(x):
    return lax.rsqrt(jnp.mean(x * x, axis=-1, keepdims=True) + EPS)


def _proj_kernel(x_ref, g1_ref, w_ref, gq_ref, gk_ref, *refs):
    later_f32 = refs[:N_LATER_WEIGHTS]
    q_ref, k_ref, v_ref, u_ref = refs[N_LATER_WEIGHTS:N_LATER_WEIGHTS + 4]
    later_bf = refs[N_LATER_WEIGHTS + 4:2 * N_LATER_WEIGHTS + 4]
    w_bf = refs[-1]

    @pl.when(pl.program_id(0) == 0)
    def _():
        w_bf[...] = w_ref[...].astype(BF16)

    for src, dst in zip(later_f32, later_bf):
        dst[...] = src[...].astype(BF16)

    x = x_ref[...]
    h = (x * _rms_scale(x) * g1_ref[...]).astype(BF16)
    proj = jnp.dot(h, w_bf[...], preferred_element_type=F32)
    first_head = lax.broadcasted_iota(jnp.int32, (1, LANES), 1) < NA_HEAD_DIM

    def head_norm(t, g):
        tiles = []
        for p in range(HEAD_PAIRS):
            tp = t[:, p * LANES:(p + 1) * LANES]
            sq = tp * tp
            ss_first = jnp.sum(jnp.where(first_head, sq, 0.0), axis=-1, keepdims=True)
            ss_second = jnp.sum(jnp.where(first_head, 0.0, sq), axis=-1, keepdims=True)
            ms = jnp.where(first_head, ss_first, ss_second) * (1.0 / NA_HEAD_DIM)
            tiles.append(tp * lax.rsqrt(ms + EPS))
        return jnp.concatenate(tiles, axis=-1) * g

    q = head_norm(proj[:, :NA_WIDTH], gq_ref[...])
    k = head_norm(proj[:, NA_WIDTH:2 * NA_WIDTH], gk_ref[...])
    q = (q * SCORE_SCALE).astype(BF16)
    k = k.astype(BF16)
    v = proj[:, 2 * NA_WIDTH:3 * NA_WIDTH].astype(BF16)
    for p in range(HEAD_PAIRS):
        lanes = slice(p * LANES, (p + 1) * LANES)
        q_ref[p] = q[:, lanes]
        k_ref[p] = k[:, lanes]
        v_ref[p] = v[:, lanes]
    u_ref[...] = proj[:, 3 * NA_WIDTH:]


def _proj_call(x2, g1, w_in, layer, gq, gk, later_weights):
    assert len(later_weights) == N_LATER_WEIGHTS
    n = x2.shape[0]
    tm = PROJ_BLOCK
    steps = n // tm
    assert steps % CAST_BANDS == 0
    steps_per_band = steps // CAST_BANDS
    const = lambda i: (0, 0)
    row = lambda i: (i, 0)
    pair_row = lambda i: (0, i, 0)
    band_in, band_out, band_shape = [], [], []
    for w in later_weights:
        rows, cols = w.shape[1:]
        assert rows % (CAST_BANDS * BF16_SUBLANES) == 0
        band_in.append(pl.BlockSpec((None, rows // CAST_BANDS, cols),
                                    lambda i: (layer, i // steps_per_band, 0)))
        band_out.append(pl.BlockSpec((rows // CAST_BANDS, cols), lambda i: (i // steps_per_band, 0)))
        band_shape.append(jax.ShapeDtypeStruct((rows, cols), BF16))
    outs = pl.pallas_call(
        _proj_kernel,
        grid=(steps,),
        in_specs=[
            pl.BlockSpec((tm, D_MODEL), row),
            pl.BlockSpec((1, D_MODEL), const),
            pl.BlockSpec((None, D_MODEL, IN_WIDTH), lambda i: (layer, 0, 0),
                         pipeline_mode=pl.Buffered(1)),
            pl.BlockSpec((1, NA_WIDTH), const),
            pl.BlockSpec((1, NA_WIDTH), const),
        ] + band_in,
        out_specs=[
            pl.BlockSpec((HEAD_PAIRS, tm, LANES), pair_row),
            pl.BlockSpec((HEAD_PAIRS, tm, LANES), pair_row),
            pl.BlockSpec((HEAD_PAIRS, tm, LANES), pair_row),
            pl.BlockSpec((tm, POOL_WIDTH), row),
        ] + band_out,
        out_shape=[
            jax.ShapeDtypeStruct((HEAD_PAIRS, n, LANES), BF16),
            jax.ShapeDtypeStruct((HEAD_PAIRS, n, LANES), BF16),
            jax.ShapeDtypeStruct((HEAD_PAIRS, n, LANES), BF16),
            jax.ShapeDtypeStruct((n, POOL_WIDTH), F32),
        ] + band_shape,
        scratch_shapes=[pltpu.VMEM((D_MODEL, IN_WIDTH), BF16)],
        compiler_params=pltpu.CompilerParams(
            dimension_semantics=("arbitrary",), vmem_limit_bytes=VMEM_LIMIT_BYTES),
        name="proj",
    )(x2, g1, w_in, gq, gk, *later_weights)
    return outs[:4], outs[4:]


def _attn_kernel(q_ref, k_ref, v_ref, bias_ref, a_ref, p_scr, *, rows):
    wr = WIN_ROWS_MAX
    nkeys = wr * GRID_W
    lane = lax.broadcasted_iota(jnp.int32, (GRID_W, LANES), 1)
    first_head = lane < NA_HEAD_DIM

    def window_start(r):
        return jnp.clip(r - wr // 2, 0, rows - wr)

    def probabilities(r, slot):
        rs = window_start(r)
        first_rel = rs - r + WIN_ROWS_MAX - 1
        q0 = pl.multiple_of(r * GRID_W, GRID_W)
        k0 = pl.multiple_of(rs * GRID_W, GRID_W)
        for pair in range(HEAD_PAIRS):
            qs = q_ref[pair, pl.ds(q0, GRID_W), :]
            zero = jnp.zeros_like(qs)
            lhs = jnp.concatenate([jnp.where(first_head, qs, zero),
                                   jnp.where(first_head, zero, qs)], axis=0)
            kw = k_ref[pair, pl.ds(k0, nkeys), :]
            s = lax.dot_general(lhs, kw, (((1,), (1,)), ((), ())),
                                preferred_element_type=F32)
            s = s + jnp.concatenate(
                [jnp.concatenate([bias_ref[2 * pair + hh, first_rel + 2 * j] for hh in range(2)], axis=0)
                 for j in range(nkeys // LANES)], axis=1)
            m = jnp.max(s, axis=-1, keepdims=True)
            p_scr[slot, pair] = jnp.exp2(s - m).astype(BF16)

    ones = jnp.ones((nkeys, LANES), BF16)

    def weighted_values(r, slot):
        q0 = pl.multiple_of(r * GRID_W, GRID_W)
        k0 = pl.multiple_of(window_start(r) * GRID_W, GRID_W)
        for pair in range(HEAD_PAIRS):
            vw = jnp.concatenate([v_ref[pair, pl.ds(k0, nkeys), :], ones], axis=1)
            ol = jnp.dot(p_scr[slot, pair], vw, preferred_element_type=F32)
            merged = jnp.where(jnp.concatenate([first_head, first_head], axis=1),
                               ol[:GRID_W], ol[GRID_W:])
            a_ref[pair, pl.ds(q0, GRID_W), :] = (merged[:, :LANES] / merged[:, LANES:]).astype(BF16)

    probabilities(0, 0)

    def row_group(j, carry):
        for k in range(ATTN_ROWS_PER_STEP):
            r = ATTN_ROWS_PER_STEP * j + k
            probabilities(jnp.minimum(r + 1, rows - 1), (k + 1) % 2)
            weighted_values(r, k % 2)
        return carry

    lax.fori_loop(0, rows // ATTN_ROWS_PER_STEP, row_group, 0)


def _attn_call(q, k, v, bias, batch, seq):
    rows = seq // GRID_W
    assert rows >= WIN_ROWS_MAX and rows % ATTN_ROWS_PER_STEP == 0
    nkeys = WIN_ROWS_MAX * GRID_W
    per_batch = lambda b: (0, b, 0)
    blk = pl.BlockSpec((HEAD_PAIRS, seq, LANES), per_batch)
    kernel = functools.partial(_attn_kernel, rows=rows)
    return pl.pallas_call(
        kernel,
        grid=(batch,),
        in_specs=[blk, blk, blk,
                  pl.BlockSpec(bias.shape, lambda b: (0, 0, 0, 0), pipeline_mode=pl.Buffered(1))],
        out_specs=blk,
        out_shape=jax.ShapeDtypeStruct(q.shape, BF16),
        scratch_shapes=[pltpu.VMEM((2, HEAD_PAIRS, 2 * GRID_W, nkeys), BF16)],
        compiler_params=pltpu.CompilerParams(
            dimension_semantics=("arbitrary",), vmem_limit_bytes=VMEM_LIMIT_BYTES),
        name="attn",
    )(q, k, v, bias)


def _pool_minus_token(ext, first_token, seq):
    n = ext.shape[0]
    half = POOL_HALO // 2
    tq = n - 2 * half

    def ahead(e, k):
        return pltpu.roll(e, n - k, axis=0)

    def window_count(t, w):
        return (jnp.minimum(t + w // 2, seq) - jnp.maximum(t - w // 2, 0)).astype(F32)

    edge = lax.broadcasted_iota(jnp.int32, (half, 1), 0)
    t_top = first_token + edge
    t_bot = first_token + tq - half + edge
    out = []
    for g, w in enumerate(POOL_WINDOWS):
        e = ext[:, g * POOL_GROUP_DIM:(g + 1) * POOL_GROUP_DIM]
        tok = e[half:half + tq]
        span = 1
        while 2 * span < w:
            e = e + ahead(e, span)
            span *= 2
        lo = half - w // 2
        first = ahead(e, lo)[:tq] if lo else e[:tq]
        s = first + e[half:half + tq]
        mean = jnp.concatenate([s[:half] / window_count(t_top, w),
                                s[half:tq - half] * (1.0 / w),
                                s[tq - half:] / window_count(t_bot, w)], axis=0)
        out.append(mean - tok)
    return jnp.concatenate(out, axis=-1)


def _tail_kernel(x_ref, a_ref, u_ref, up_ref, un_ref, wpool_ref, pscale_ref, wout_ref,
                 g2_ref, wg_ref, wu_ref, wd_ref, o_ref, *, seq, blocks_per_seq):
    i = pl.program_id(0) % blocks_per_seq
    tq = x_ref.shape[0]
    half = POOL_HALO // 2

    attn = jnp.concatenate([a_ref[p] for p in range(HEAD_PAIRS)], axis=-1)
    x1 = x_ref[...] + jnp.dot(attn, wout_ref[:NA_WIDTH, :], preferred_element_type=F32)

    prev_ok = (i > 0).astype(F32)
    next_ok = (i < blocks_per_seq - 1).astype(F32)
    ext = jnp.concatenate([up_ref[half:, :] * prev_ok, u_ref[...],
                           un_ref[:half, :] * next_ok], axis=0)
    d = _pool_minus_token(ext, i * tq, seq).astype(BF16)
    y = jnp.dot(d, wpool_ref[...], preferred_element_type=F32) * pscale_ref[...]
    x1 = x1 + jnp.dot(y.astype(BF16), wout_ref[NA_WIDTH:, :], preferred_element_type=F32)

    h = (x1 * _rms_scale(x1) * g2_ref[...]).astype(BF16)
    d_ff = wd_ref.shape[0]
    ffn = None
    for f0 in range(0, d_ff, FFN_CHUNK):
        f1 = min(f0 + FFN_CHUNK, d_ff)
        gate = jnp.dot(h, wg_ref[:, f0:f1], preferred_element_type=F32)
        up = jnp.dot(h, wu_ref[:, f0:f1], preferred_element_type=F32)
        act = (gate * jax.nn.sigmoid(gate) * up).astype(BF16)
        part = jnp.dot(act, wd_ref[f0:f1, :], preferred_element_type=F32)
        ffn = part if ffn is None else ffn + part
    o_ref[...] = x1 + ffn


def _tail_call(x2, a, u, wpool_bd, pscale, w_out, g2, w_gate, w_up, w_down, seq):
    n = x2.shape[0]
    tq = TAIL_BLOCK
    d_ff = w_gate.shape[1]
    nblk = seq // tq
    halo_per_blk = tq // POOL_HALO
    n_halo = seq // POOL_HALO
    tok = lambda g: (g, 0)
    const = lambda g: (0, 0)
    prev_halo = lambda g: ((g // nblk) * n_halo + jnp.maximum((g % nblk) * halo_per_blk - 1, 0), 0)
    next_halo = lambda g: ((g // nblk) * n_halo
                           + jnp.minimum((g % nblk + 1) * halo_per_blk, n_halo - 1), 0)
    resident = pl.Buffered(1)
    kernel = functools.partial(_tail_kernel, seq=seq, blocks_per_seq=nblk)
    return pl.pallas_call(
        kernel,
        grid=(n // tq,),
        in_specs=[
            pl.BlockSpec((tq, D_MODEL), tok),
            pl.BlockSpec((HEAD_PAIRS, tq, LANES), lambda g: (0, g, 0)),
            pl.BlockSpec((tq, POOL_WIDTH), tok),
            pl.BlockSpec((POOL_HALO, POOL_WIDTH), prev_halo),
            pl.BlockSpec((POOL_HALO, POOL_WIDTH), next_halo),
            pl.BlockSpec((POOL_WIDTH, POOL_WIDTH), const, pipeline_mode=resident),
            pl.BlockSpec((1, POOL_WIDTH), const),
            pl.BlockSpec((D_MODEL, D_MODEL), const, pipeline_mode=resident),
            pl.BlockSpec((1, D_MODEL), const),
            pl.BlockSpec((D_MODEL, d_ff), const, pipeline_mode=resident),
            pl.BlockSpec((D_MODEL, d_ff), const, pipeline_mode=resident),
            pl.BlockSpec((d_ff, D_MODEL), const, pipeline_mode=resident),
        ],
        out_specs=pl.BlockSpec((tq, D_MODEL), tok),
        out_shape=jax.ShapeDtypeStruct(x2.shape, F32),
        compiler_params=pltpu.CompilerParams(
            dimension_semantics=("arbitrary",), vmem_limit_bytes=VMEM_LIMIT_BYTES),
        name="tail",
    )(x2, a, u, u, u, wpool_bd, pscale, w_out, g2, w_gate, w_up, w_down)


def _bias_table(rpb):
    c = np.arange(GRID_W)
    col_start = np.clip(c - WIN_COLS // 2, 0, GRID_W - WIN_COLS)
    kc = np.arange(2 * GRID_W) % GRID_W
    inside = (kc[None, :] >= col_start[:, None]) & (kc[None, :] < col_start[:, None] + WIN_COLS)
    col_rel = kc[None, :] - c[:, None] + WIN_COLS - 1
    onehot = (col_rel[:, :, None] == np.arange(2 * WIN_COLS - 1)) & inside[:, :, None]
    both = jnp.einsum('hrj,cmj->hrcm', rpb.astype(F32), jnp.asarray(onehot, F32),
                      precision=lax.Precision.HIGHEST)
    low_lanes = np.arange(2 * GRID_W) < GRID_W
    tbl = jnp.where(low_lanes, both[:, :-1], both[:, 1:])
    return jnp.where(inside[None, None], tbl, MASK_VALUE)


def _block_diag(w_pool):
    g, d, _ = w_pool.shape
    eye = jnp.eye(g, dtype=w_pool.dtype)
    return jnp.einsum('gcd,gh->gchd', w_pool, eye).reshape(g * d, g * d)


def kernel(x, norm1_g, w_in, q_norm_g, k_norm_g, rpb, w_pool, pool_scale, w_out,
           norm2_g, w_gate, w_up, w_down):
    batch, seq, _ = x.shape
    depth = w_in.shape[0]
    x2 = x.reshape(batch * seq, D_MODEL)
    for l in range(depth):
        gq = jnp.tile(q_norm_g[l], NA_HEADS)[None, :]
        gk = jnp.tile(k_norm_g[l], NA_HEADS)[None, :]
        (q, k, v, u), (w_out_bf, w_gate_bf, w_up_bf, w_down_bf) = _proj_call(
            x2, norm1_g[l][None, :], w_in, l, gq, gk, (w_out, w_gate, w_up, w_down))
        a = _attn_call(q, k, v, _bias_table(rpb[l] * LOG2_E), batch, seq)
        x2 = _tail_call(x2, a, u, _block_diag(w_pool[l]).astype(BF16), pool_scale[l][None, :],
                        w_out_bf, norm2_g[l][None, :], w_gate_bf, w_up_bf, w_down_bf, seq)
    return x2.reshape(batch, seq, D_MODEL)
```

```python
import functools

import jax
import jax.numpy as jnp
import numpy as np
from jax import lax
from jax.experimental import pallas as pl
from jax.experimental.pallas import tpu as pltpu

D_MODEL = 1024
GRID_W = 64
NA_WIDTH = D_MODEL // 2
NA_HEADS = 8
NA_HEAD_DIM = NA_WIDTH // NA_HEADS
WIN_ROWS_MAX = 8
WIN_COLS = 16
POOL_WIDTH = D_MODEL - NA_WIDTH
POOL_WINDOWS = (2, 4, 8, 16)
POOL_GROUPS = len(POOL_WINDOWS)
POOL_GROUP_DIM = POOL_WIDTH // POOL_GROUPS
IN_WIDTH = 3 * NA_WIDTH + POOL_WIDTH
EPS = 1e-6

LANES = 128
HEAD_PAIRS = NA_WIDTH // LANES
MASK_VALUE = -1e30
LOG2_E = 1.4426950408889634
SCORE_SCALE = NA_HEAD_DIM ** -0.5 * LOG2_E
POOL_HALO = 16
VMEM_LIMIT_BYTES = 56 * 1024 * 1024

PROJ_BLOCK = 1024
N_LATER_WEIGHTS = 4
CAST_BANDS = 16
BF16_SUBLANES = 16
ATTN_ROWS_PER_STEP = 32
FFN_CHUNK = 1024
TAIL_BLOCK = 1024

F32 = jnp.float32
BF16 = jnp.bfloat16


def _rms_scale(x):
    return lax.rsqrt(jnp.mean(x * x, axis=-1, keepdims=True) + EPS)


def _proj_kernel(x_ref, g1_ref, w_ref, gq_ref, gk_ref, *refs):
    later_f32 = refs[:N_LATER_WEIGHTS]
    q_ref, k_ref, v_ref, u_ref = refs[N_LATER_WEIGHTS:N_LATER_WEIGHTS + 4]
    later_bf = refs[N_LATER_WEIGHTS + 4:2 * N_LATER_WEIGHTS + 4]
    w_bf = refs[-1]

    @pl.when(pl.program_id(0) == 0)
    def _():
        w_bf[...] = w_ref[...].astype(BF16)

    for src, dst in zip(later_f32, later_bf):
        dst[...] = src[...].astype(BF16)

    x = x_ref[...]
    h = (x * _rms_scale(x) * g1_ref[...]).astype(BF16)
    proj = jnp.dot(h, w_bf[...], preferred_element_type=F32)
    first_head = lax.broadcasted_iota(jnp.int32, (1, LANES), 1) < NA_HEAD_DIM

    def head_norm(t, g):
        tiles = []
        for p in range(HEAD_PAIRS):
            tp = t[:, p * LANES:(p + 1) * LANES]
            sq = tp * tp
            ss_first = jnp.sum(jnp.where(first_head, sq, 0.0), axis=-1, keepdims=True)
            ss_second = jnp.sum(jnp.where(first_head, 0.0, sq), axis=-1, keepdims=True)
            ms = jnp.where(first_head, ss_first, ss_second) * (1.0 / NA_HEAD_DIM)
            tiles.append(tp * lax.rsqrt(ms + EPS))
        return jnp.concatenate(tiles, axis=-1) * g

    q = head_norm(proj[:, :NA_WIDTH], gq_ref[...])
    k = head_norm(proj[:, NA_WIDTH:2 * NA_WIDTH], gk_ref[...])
    q = (q * SCORE_SCALE).astype(BF16)
    k = k.astype(BF16)
    v = proj[:, 2 * NA_WIDTH:3 * NA_WIDTH].astype(BF16)
    for p in range(HEAD_PAIRS):
        lanes = slice(p * LANES, (p + 1) * LANES)
        q_ref[p] = q[:, lanes]
        k_ref[p] = k[:, lanes]
        v_ref[p] = v[:, lanes]
    u_ref[...] = proj[:, 3 * NA_WIDTH:]


def _proj_call(x2, g1, w_in, layer, gq, gk, later_weights):
    assert len(later_weights) == N_LATER_WEIGHTS
    n = x2.shape[0]
    tm = PROJ_BLOCK
    steps = n // tm
    assert steps % CAST_BANDS == 0
    steps_per_band = steps // CAST_BANDS
    const = lambda i: (0, 0)
    row = lambda i: (i, 0)
    pair_row = lambda i: (0, i, 0)
    band_in, band_out, band_shape = [], [], []
    for w in later_weights:
        rows, cols = w.shape[1:]
        assert rows % (CAST_BANDS * BF16_SUBLANES) == 0
        band_in.append(pl.BlockSpec((None, rows // CAST_BANDS, cols),
                                    lambda i: (layer, i // steps_per_band, 0)))
        band_out.append(pl.BlockSpec((rows // CAST_BANDS, cols), lambda i: (i // steps_per_band, 0)))
        band_shape.append(jax.ShapeDtypeStruct((rows, cols), BF16))
    outs = pl.pallas_call(
        _proj_kernel,
        grid=(steps,),
        in_specs=[
            pl.BlockSpec((tm, D_MODEL), row),
            pl.BlockSpec((1, D_MODEL), const),
            pl.BlockSpec((None, D_MODEL, IN_WIDTH), lambda i: (layer, 0, 0),
                         pipeline_mode=pl.Buffered(1)),
            pl.BlockSpec((1, NA_WIDTH), const),
            pl.BlockSpec((1, NA_WIDTH), const),
        ] + band_in,
        out_specs=[
            pl.BlockSpec((HEAD_PAIRS, tm, LANES), pair_row),
            pl.BlockSpec((HEAD_PAIRS, tm, LANES), pair_row),
            pl.BlockSpec((HEAD_PAIRS, tm, LANES), pair_row),
            pl.BlockSpec((tm, POOL_WIDTH), row),
        ] + band_out,
        out_shape=[
            jax.ShapeDtypeStruct((HEAD_PAIRS, n, LANES), BF16),
            jax.ShapeDtypeStruct((HEAD_PAIRS, n, LANES), BF16),
            jax.ShapeDtypeStruct((HEAD_PAIRS, n, LANES), BF16),
            jax.ShapeDtypeStruct((n, POOL_WIDTH), F32),
        ] + band_shape,
        scratch_shapes=[pltpu.VMEM((D_MODEL, IN_WIDTH), BF16)],
        compiler_params=pltpu.CompilerParams(
            dimension_semantics=("arbitrary",), vmem_limit_bytes=VMEM_LIMIT_BYTES),
        name="proj",
    )(x2, g1, w_in, gq, gk, *later_weights)
    return outs[:4], outs[4:]


def _attn_kernel(q_ref, k_ref, v_ref, bias_ref, a_ref, *, rows):
    wr = WIN_ROWS_MAX
    nkeys = wr * GRID_W
    lane = lax.broadcasted_iota(jnp.int32, (GRID_W, LANES), 1)
    first_head = lane < NA_HEAD_DIM

    def window_start(r):
        return jnp.clip(r - wr // 2, 0, rows - wr)

    def probabilities(r, pair):
        rs = window_start(r)
        first_rel = rs - r + WIN_ROWS_MAX - 1
        q0 = pl.multiple_of(r * GRID_W, GRID_W)
        k0 = pl.multiple_of(rs * GRID_W, GRID_W)
        qs = q_ref[pair, pl.ds(q0, GRID_W), :]
        zero = jnp.zeros_like(qs)
        lhs = jnp.concatenate([jnp.where(first_head, qs, zero),
                               jnp.where(first_head, zero, qs)], axis=0)
        kw = k_ref[pair, pl.ds(k0, nkeys), :]
        s = lax.dot_general(lhs, kw, (((1,), (1,)), ((), ())),
                            preferred_element_type=F32)
        s = s + jnp.concatenate(
            [jnp.concatenate([bias_ref[2 * pair + hh, first_rel + 2 * j] for hh in range(2)], axis=0)
             for j in range(nkeys // LANES)], axis=1)
        m = jnp.max(s, axis=-1, keepdims=True)
        return jnp.exp2(s - m).astype(BF16)

    ones = jnp.ones((nkeys, LANES), BF16)

    def weighted_values(r, pair, p):
        q0 = pl.multiple_of(r * GRID_W, GRID_W)
        k0 = pl.multiple_of(window_start(r) * GRID_W, GRID_W)
        vw = jnp.concatenate([v_ref[pair, pl.ds(k0, nkeys), :], ones], axis=1)
        ol = jnp.dot(p, vw, preferred_element_type=F32)
        merged = jnp.where(jnp.concatenate([first_head, first_head], axis=1),
                           ol[:GRID_W], ol[GRID_W:])
        a_ref[pair, pl.ds(q0, GRID_W), :] = (merged[:, :LANES] / merged[:, LANES:]).astype(BF16)

    def row_group(j, carry):
        tiles = [(ATTN_ROWS_PER_STEP * j + k, pair)
                 for k in range(ATTN_ROWS_PER_STEP) for pair in range(HEAD_PAIRS)]
        p_next = probabilities(*tiles[0])
        for t, tile in enumerate(tiles):
            p = p_next
            if t + 1 < len(tiles):
                p_next = probabilities(*tiles[t + 1])
            weighted_values(*tile, p)
        return carry

    lax.fori_loop(0, rows // ATTN_ROWS_PER_STEP, row_group, 0)


def _attn_call(q, k, v, bias, batch, seq):
    rows = seq // GRID_W
    assert rows >= WIN_ROWS_MAX and rows % ATTN_ROWS_PER_STEP == 0
    per_batch = lambda b: (0, b, 0)
    blk = pl.BlockSpec((HEAD_PAIRS, seq, LANES), per_batch)
    kernel = functools.partial(_attn_kernel, rows=rows)
    return pl.pallas_call(
        kernel,
        grid=(batch,),
        in_specs=[blk, blk, blk,
                  pl.BlockSpec(bias.shape, lambda b: (0, 0, 0, 0), pipeline_mode=pl.Buffered(1))],
        out_specs=blk,
        out_shape=jax.ShapeDtypeStruct(q.shape, BF16),
        compiler_params=pltpu.CompilerParams(
            dimension_semantics=("arbitrary",), vmem_limit_bytes=VMEM_LIMIT_BYTES),
        name="attn",
    )(q, k, v, bias)


def _pool_minus_token(ext, first_token, seq):
    n = ext.shape[0]
    half = POOL_HALO // 2
    tq = n - 2 * half

    def ahead(e, k):
        return pltpu.roll(e, n - k, axis=0)

    def window_count(t, w):
        return (jnp.minimum(t + w // 2, seq) - jnp.maximum(t - w // 2, 0)).astype(F32)

    edge = lax.broadcasted_iota(jnp.int32, (half, 1), 0)
    t_top = first_token + edge
    t_bot = first_token + tq - half + edge
    out = []
    for g, w in enumerate(POOL_WINDOWS):
        e = ext[:, g * POOL_GROUP_DIM:(g + 1) * POOL_GROUP_DIM]
        tok = e[half:half + tq]
        span = 1
        while 2 * span < w:
            e = e + ahead(e, span)
            span *= 2
        lo = half - w // 2
        first = ahead(e, lo)[:tq] if lo else e[:tq]
        s = first + e[half:half + tq]
        mean = jnp.concatenate([s[:half] / window_count(t_top, w),
                                s[half:tq - half] * (1.0 / w),
                                s[tq - half:] / window_count(t_bot, w)], axis=0)
        out.append(mean - tok)
    return jnp.concatenate(out, axis=-1)


def _tail_kernel(x_ref, a_ref, u_ref, up_ref, un_ref, wpool_ref, pscale_ref, wout_ref,
                 g2_ref, wg_ref, wu_ref, wd_ref, o_ref, *, seq, blocks_per_seq):
    i = pl.program_id(0) % blocks_per_seq
    tq = x_ref.shape[0]
    half = POOL_HALO // 2

    attn = jnp.concatenate([a_ref[p] for p in range(HEAD_PAIRS)], axis=-1)
    x1 = x_ref[...] + jnp.dot(attn, wout_ref[:NA_WIDTH, :], preferred_element_type=F32)

    prev_ok = (i > 0).astype(F32)
    next_ok = (i < blocks_per_seq - 1).astype(F32)
    ext = jnp.concatenate([up_ref[half:, :] * prev_ok, u_ref[...],
                           un_ref[:half, :] * next_ok], axis=0)
    d = _pool_minus_token(ext, i * tq, seq).astype(BF16)
    y = jnp.dot(d, wpool_ref[...], preferred_element_type=F32) * pscale_ref[...]
    x1 = x1 + jnp.dot(y.astype(BF16), wout_ref[NA_WIDTH:, :], preferred_element_type=F32)

    h = (x1 * _rms_scale(x1) * g2_ref[...]).astype(BF16)
    d_ff = wd_ref.shape[0]
    ffn = None
    for f0 in range(0, d_ff, FFN_CHUNK):
        f1 = min(f0 + FFN_CHUNK, d_ff)
        gate = jnp.dot(h, wg_ref[:, f0:f1], preferred_element_type=F32)
        up = jnp.dot(h, wu_ref[:, f0:f1], preferred_element_type=F32)
        act = (gate * jax.nn.sigmoid(gate) * up).astype(BF16)
        part = jnp.dot(act, wd_ref[f0:f1, :], preferred_element_type=F32)
        ffn = part if ffn is None else ffn + part
    o_ref[...] = x1 + ffn


def _tail_call(x2, a, u, wpool_bd, pscale, w_out, g2, w_gate, w_up, w_down, seq):
    n = x2.shape[0]
    tq = TAIL_BLOCK
    d_ff = w_gate.shape[1]
    nblk = seq // tq
    halo_per_blk = tq // POOL_HALO
    n_halo = seq // POOL_HALO
    tok = lambda g: (g, 0)
    const = lambda g: (0, 0)
    prev_halo = lambda g: ((g // nblk) * n_halo + jnp.maximum((g % nblk) * halo_per_blk - 1, 0), 0)
    next_halo = lambda g: ((g // nblk) * n_halo
                           + jnp.minimum((g % nblk + 1) * halo_per_blk, n_halo - 1), 0)
    resident = pl.Buffered(1)
    kernel = functools.partial(_tail_kernel, seq=seq, blocks_per_seq=nblk)
    return pl.pallas_call(
        kernel,
        grid=(n // tq,),
        in_specs=[
            pl.BlockSpec((tq, D_MODEL), tok),
            pl.BlockSpec((HEAD_PAIRS, tq, LANES), lambda g: (0, g, 0)),
            pl.BlockSpec((tq, POOL_WIDTH), tok),
            pl.BlockSpec((POOL_HALO, POOL_WIDTH), prev_halo),
            pl.BlockSpec((POOL_HALO, POOL_WIDTH), next_halo),
            pl.BlockSpec((POOL_WIDTH, POOL_WIDTH), const, pipeline_mode=resident),
            pl.BlockSpec((1, POOL_WIDTH), const),
            pl.BlockSpec((D_MODEL, D_MODEL), const, pipeline_mode=resident),
            pl.BlockSpec((1, D_MODEL), const),
            pl.BlockSpec((D_MODEL, d_ff), const, pipeline_mode=resident),
            pl.BlockSpec((D_MODEL, d_ff), const, pipeline_mode=resident),
            pl.BlockSpec((d_ff, D_MODEL), const, pipeline_mode=resident),
        ],
        out_specs=pl.BlockSpec((tq, D_MODEL), tok),
        out_shape=jax.ShapeDtypeStruct(x2.shape, F32),
        compiler_params=pltpu.CompilerParams(
            dimension_semantics=("arbitrary",), vmem_limit_bytes=VMEM_LIMIT_BYTES),
        name="tail",
    )(x2, a, u, u, u, wpool_bd, pscale, w_out, g2, w_gate, w_up, w_down)


def _bias_table(rpb):
    c = np.arange(GRID_W)
    col_start = np.clip(c - WIN_COLS // 2, 0, GRID_W - WIN_COLS)
    kc = np.arange(2 * GRID_W) % GRID_W
    inside = (kc[None, :] >= col_start[:, None]) & (kc[None, :] < col_start[:, None] + WIN_COLS)
    col_rel = kc[None, :] - c[:, None] + WIN_COLS - 1
    onehot = (col_rel[:, :, None] == np.arange(2 * WIN_COLS - 1)) & inside[:, :, None]
    both = jnp.einsum('hrj,cmj->hrcm', rpb.astype(F32), jnp.asarray(onehot, F32),
                      precision=lax.Precision.HIGHEST)
    low_lanes = np.arange(2 * GRID_W) < GRID_W
    tbl = jnp.where(low_lanes, both[:, :-1], both[:, 1:])
    return jnp.where(inside[None, None], tbl, MASK_VALUE)


def _block_diag(w_pool):
    g, d, _ = w_pool.shape
    eye = jnp.eye(g, dtype=w_pool.dtype)
    return jnp.einsum('gcd,gh->gchd', w_pool, eye).reshape(g * d, g * d)


def kernel(x, norm1_g, w_in, q_norm_g, k_norm_g, rpb, w_pool, pool_scale, w_out,
           norm2_g, w_gate, w_up, w_down):
    batch, seq, _ = x.shape
    depth = w_in.shape[0]
    x2 = x.reshape(batch * seq, D_MODEL)
    for l in range(depth):
        gq = jnp.tile(q_norm_g[l], NA_HEADS)[None, :]
        gk = jnp.tile(k_norm_g[l], NA_HEADS)[None, :]
        (q, k, v, u), (w_out_bf, w_gate_bf, w_up_bf, w_down_bf) = _proj_call(
            x2, norm1_g[l][None, :], w_in, l, gq, gk, (w_out, w_gate, w_up, w_down))
        a = _attn_call(q, k, v, _bias_table(rpb[l] * LOG2_E), batch, seq)
        x2 = _tail_call(x2, a, u, _block_diag(w_pool[l]).astype(BF16), pool_scale[l][None, :],
                        w_out_bf, norm2_g[l][None, :], w_gate_bf, w_up_bf, w_down_bf, seq)
    return x2.reshape(batch, seq, D_MODEL)
```

```python
import functools

import jax
import jax.numpy as jnp
import numpy as np
from jax import lax
from jax.experimental import pallas as pl
from jax.experimental.pallas import tpu as pltpu

D_MODEL = 1024
GRID_W = 64
NA_WIDTH = D_MODEL // 2
NA_HEADS = 8
NA_HEAD_DIM = NA_WIDTH // NA_HEADS
WIN_ROWS_MAX = 8
WIN_COLS = 16
POOL_WIDTH = D_MODEL - NA_WIDTH
POOL_WINDOWS = (2, 4, 8, 16)
POOL_GROUPS = len(POOL_WINDOWS)
POOL_GROUP_DIM = POOL_WIDTH // POOL_GROUPS
IN_WIDTH = 3 * NA_WIDTH + POOL_WIDTH
EPS = 1e-6

LANES = 128
HEAD_PAIRS = NA_WIDTH // LANES
MASK_VALUE = -1e30
LOG2_E = 1.4426950408889634
SCORE_SCALE = NA_HEAD_DIM ** -0.5 * LOG2_E
POOL_HALO = 16
VMEM_LIMIT_BYTES = 56 * 1024 * 1024

PROJ_BLOCK = 1024
N_LATER_WEIGHTS = 4
CAST_BANDS = 16
BF16_SUBLANES = 16
ATTN_ROWS_PER_STEP = 32
ATTN_LOOKAHEAD = 2
FFN_CHUNK = 1024
TAIL_BLOCK = 1024

F32 = jnp.float32
BF16 = jnp.bfloat16


def _rms_scale(x):
    return lax.rsqrt(jnp.mean(x * x, axis=-1, keepdims=True) + EPS)


def _proj_kernel(x_ref, g1_ref, w_ref, gq_ref, gk_ref, *refs):
    later_f32 = refs[:N_LATER_WEIGHTS]
    q_ref, k_ref, v_ref, u_ref = refs[N_LATER_WEIGHTS:N_LATER_WEIGHTS + 4]
    later_bf = refs[N_LATER_WEIGHTS + 4:2 * N_LATER_WEIGHTS + 4]
    w_bf = refs[-1]

    @pl.when(pl.program_id(0) == 0)
    def _():
        w_bf[...] = w_ref[...].astype(BF16)

    for src, dst in zip(later_f32, later_bf):
        dst[...] = src[...].astype(BF16)

    x = x_ref[...]
    h = (x * _rms_scale(x) * g1_ref[...]).astype(BF16)
    proj = jnp.dot(h, w_bf[...], preferred_element_type=F32)
    first_head = lax.broadcasted_iota(jnp.int32, (1, LANES), 1) < NA_HEAD_DIM

    def head_norm(t, g):
        tiles = []
        for p in range(HEAD_PAIRS):
            tp = t[:, p * LANES:(p + 1) * LANES]
            sq = tp * tp
            ss_first = jnp.sum(jnp.where(first_head, sq, 0.0), axis=-1, keepdims=True)
            ss_second = jnp.sum(jnp.where(first_head, 0.0, sq), axis=-1, keepdims=True)
            ms = jnp.where(first_head, ss_first, ss_second) * (1.0 / NA_HEAD_DIM)
            tiles.append(tp * lax.rsqrt(ms + EPS))
        return jnp.concatenate(tiles, axis=-1) * g

    q = head_norm(proj[:, :NA_WIDTH], gq_ref[...])
    k = head_norm(proj[:, NA_WIDTH:2 * NA_WIDTH], gk_ref[...])
    q = (q * SCORE_SCALE).astype(BF16)
    k = k.astype(BF16)
    v = proj[:, 2 * NA_WIDTH:3 * NA_WIDTH].astype(BF16)
    for p in range(HEAD_PAIRS):
        lanes = slice(p * LANES, (p + 1) * LANES)
        q_ref[p] = q[:, lanes]
        k_ref[p] = k[:, lanes]
        v_ref[p] = v[:, lanes]
    u_ref[...] = proj[:, 3 * NA_WIDTH:]


def _proj_call(x2, g1, w_in, layer, gq, gk, later_weights):
    assert len(later_weights) == N_LATER_WEIGHTS
    n = x2.shape[0]
    tm = PROJ_BLOCK
    steps = n // tm
    assert steps % CAST_BANDS == 0
    steps_per_band = steps // CAST_BANDS
    const = lambda i: (0, 0)
    row = lambda i: (i, 0)
    pair_row = lambda i: (0, i, 0)
    band_in, band_out, band_shape = [], [], []
    for w in later_weights:
        rows, cols = w.shape[1:]
        assert rows % (CAST_BANDS * BF16_SUBLANES) == 0
        band_in.append(pl.BlockSpec((None, rows // CAST_BANDS, cols),
                                    lambda i: (layer, i // steps_per_band, 0)))
        band_out.append(pl.BlockSpec((rows // CAST_BANDS, cols), lambda i: (i // steps_per_band, 0)))
        band_shape.append(jax.ShapeDtypeStruct((rows, cols), BF16))
    outs = pl.pallas_call(
        _proj_kernel,
        grid=(steps,),
        in_specs=[
            pl.BlockSpec((tm, D_MODEL), row),
            pl.BlockSpec((1, D_MODEL), const),
            pl.BlockSpec((None, D_MODEL, IN_WIDTH), lambda i: (layer, 0, 0),
                         pipeline_mode=pl.Buffered(1)),
            pl.BlockSpec((1, NA_WIDTH), const),
            pl.BlockSpec((1, NA_WIDTH), const),
        ] + band_in,
        out_specs=[
            pl.BlockSpec((HEAD_PAIRS, tm, LANES), pair_row),
            pl.BlockSpec((HEAD_PAIRS, tm, LANES), pair_row),
            pl.BlockSpec((HEAD_PAIRS, tm, LANES), pair_row),
            pl.BlockSpec((tm, POOL_WIDTH), row),
        ] + band_out,
        out_shape=[
            jax.ShapeDtypeStruct((HEAD_PAIRS, n, LANES), BF16),
            jax.ShapeDtypeStruct((HEAD_PAIRS, n, LANES), BF16),
            jax.ShapeDtypeStruct((HEAD_PAIRS, n, LANES), BF16),
            jax.ShapeDtypeStruct((n, POOL_WIDTH), F32),
        ] + band_shape,
        scratch_shapes=[pltpu.VMEM((D_MODEL, IN_WIDTH), BF16)],
        compiler_params=pltpu.CompilerParams(
            dimension_semantics=("arbitrary",), vmem_limit_bytes=VMEM_LIMIT_BYTES),
        name="proj",
    )(x2, g1, w_in, gq, gk, *later_weights)
    return outs[:4], outs[4:]


def _attn_kernel(q_ref, k_ref, v_ref, bias_ref, a_ref, *, rows):
    wr = WIN_ROWS_MAX
    nkeys = wr * GRID_W
    lane = lax.broadcasted_iota(jnp.int32, (GRID_W, LANES), 1)
    first_head = lane < NA_HEAD_DIM

    def window_start(r):
        return jnp.clip(r - wr // 2, 0, rows - wr)

    def probabilities(r, pair):
        rs = window_start(r)
        first_rel = rs - r + WIN_ROWS_MAX - 1
        q0 = pl.multiple_of(r * GRID_W, GRID_W)
        k0 = pl.multiple_of(rs * GRID_W, GRID_W)
        qs = q_ref[pair, pl.ds(q0, GRID_W), :]
        zero = jnp.zeros_like(qs)
        lhs = jnp.concatenate([jnp.where(first_head, qs, zero),
                               jnp.where(first_head, zero, qs)], axis=0)
        kw = k_ref[pair, pl.ds(k0, nkeys), :]
        s = lax.dot_general(lhs, kw, (((1,), (1,)), ((), ())),
                            preferred_element_type=F32)
        s = s + jnp.concatenate(
            [jnp.concatenate([bias_ref[2 * pair + hh, first_rel + 2 * j] for hh in range(2)], axis=0)
             for j in range(nkeys // LANES)], axis=1)
        m = jnp.max(s, axis=-1, keepdims=True)
        return jnp.exp2(s - m).astype(BF16)

    ones = jnp.ones((nkeys, LANES), BF16)

    def weighted_values(r, pair, p):
        q0 = pl.multiple_of(r * GRID_W, GRID_W)
        k0 = pl.multiple_of(window_start(r) * GRID_W, GRID_W)
        vw = jnp.concatenate([v_ref[pair, pl.ds(k0, nkeys), :], ones], axis=1)
        ol = jnp.dot(p, vw, preferred_element_type=F32)
        merged = jnp.where(jnp.concatenate([first_head, first_head], axis=1),
                           ol[:GRID_W], ol[GRID_W:])
        a_ref[pair, pl.ds(q0, GRID_W), :] = (merged[:, :LANES] / merged[:, LANES:]).astype(BF16)

    def row_group(j, carry):
        tiles = [(ATTN_ROWS_PER_STEP * j + k, pair)
                 for k in range(ATTN_ROWS_PER_STEP) for pair in range(HEAD_PAIRS)]
        ahead = [probabilities(*tiles[t]) for t in range(ATTN_LOOKAHEAD)]
        for t, tile in enumerate(tiles):
            if t + ATTN_LOOKAHEAD < len(tiles):
                ahead.append(probabilities(*tiles[t + ATTN_LOOKAHEAD]))
            weighted_values(*tile, ahead.pop(0))
        return carry

    lax.fori_loop(0, rows // ATTN_ROWS_PER_STEP, row_group, 0)


def _attn_call(q, k, v, bias, batch, seq):
    rows = seq // GRID_W
    assert rows >= WIN_ROWS_MAX and rows % ATTN_ROWS_PER_STEP == 0
    per_batch = lambda b: (0, b, 0)
    blk = pl.BlockSpec((HEAD_PAIRS, seq, LANES), per_batch)
    kernel = functools.partial(_attn_kernel, rows=rows)
    return pl.pallas_call(
        kernel,
        grid=(batch,),
        in_specs=[blk, blk, blk,
                  pl.BlockSpec(bias.shape, lambda b: (0, 0, 0, 0), pipeline_mode=pl.Buffered(1))],
        out_specs=blk,
        out_shape=jax.ShapeDtypeStruct(q.shape, BF16),
        compiler_params=pltpu.CompilerParams(
            dimension_semantics=("arbitrary",), vmem_limit_bytes=VMEM_LIMIT_BYTES),
        name="attn",
    )(q, k, v, bias)


def _pool_minus_token(ext, first_token, seq):
    n = ext.shape[0]
    half = POOL_HALO // 2
    tq = n - 2 * half

    def ahead(e, k):
        return pltpu.roll(e, n - k, axis=0)

    def window_count(t, w):
        return (jnp.minimum(t + w // 2, seq) - jnp.maximum(t - w // 2, 0)).astype(F32)

    edge = lax.broadcasted_iota(jnp.int32, (half, 1), 0)
    t_top = first_token + edge
    t_bot = first_token + tq - half + edge
    out = []
    for g, w in enumerate(POOL_WINDOWS):
        e = ext[:, g * POOL_GROUP_DIM:(g + 1) * POOL_GROUP_DIM]
        tok = e[half:half + tq]
        span = 1
        while 2 * span < w:
            e = e + ahead(e, span)
            span *= 2
        lo = half - w // 2
        first = ahead(e, lo)[:tq] if lo else e[:tq]
        s = first + e[half:half + tq]
        mean = jnp.concatenate([s[:half] / window_count(t_top, w),
                                s[half:tq - half] * (1.0 / w),
                                s[tq - half:] / window_count(t_bot, w)], axis=0)
        out.append(mean - tok)
    return jnp.concatenate(out, axis=-1)


def _tail_kernel(x_ref, a_ref, u_ref, up_ref, un_ref, wpool_ref, pscale_ref, wout_ref,
                 g2_ref, wg_ref, wu_ref, wd_ref, o_ref, *, seq, blocks_per_seq):
    i = pl.program_id(0) % blocks_per_seq
    tq = x_ref.shape[0]
    half = POOL_HALO // 2

    attn = jnp.concatenate([a_ref[p] for p in range(HEAD_PAIRS)], axis=-1)
    x1 = x_ref[...] + jnp.dot(attn, wout_ref[:NA_WIDTH, :], preferred_element_type=F32)

    prev_ok = (i > 0).astype(F32)
    next_ok = (i < blocks_per_seq - 1).astype(F32)
    ext = jnp.concatenate([up_ref[half:, :] * prev_ok, u_ref[...],
                           un_ref[:half, :] * next_ok], axis=0)
    d = _pool_minus_token(ext, i * tq, seq).astype(BF16)
    y = jnp.dot(d, wpool_ref[...], preferred_element_type=F32) * pscale_ref[...]
    x1 = x1 + jnp.dot(y.astype(BF16), wout_ref[NA_WIDTH:, :], preferred_element_type=F32)

    h = (x1 * _rms_scale(x1) * g2_ref[...]).astype(BF16)
    d_ff = wd_ref.shape[0]
    ffn = None
    for f0 in range(0, d_ff, FFN_CHUNK):
        f1 = min(f0 + FFN_CHUNK, d_ff)
        gate = jnp.dot(h, wg_ref[:, f0:f1], preferred_element_type=F32)
        up = jnp.dot(h, wu_ref[:, f0:f1], preferred_element_type=F32)
        act = (gate * jax.nn.sigmoid(gate) * up).astype(BF16)
        part = jnp.dot(act, wd_ref[f0:f1, :], preferred_element_type=F32)
        ffn = part if ffn is None else ffn + part
    o_ref[...] = x1 + ffn


def _tail_call(x2, a, u, wpool_bd, pscale, w_out, g2, w_gate, w_up, w_down, seq):
    n = x2.shape[0]
    tq = TAIL_BLOCK
    d_ff = w_gate.shape[1]
    nblk = seq // tq
    halo_per_blk = tq // POOL_HALO
    n_halo = seq // POOL_HALO
    tok = lambda g: (g, 0)
    const = lambda g: (0, 0)
    prev_halo = lambda g: ((g // nblk) * n_halo + jnp.maximum((g % nblk) * halo_per_blk - 1, 0), 0)
    next_halo = lambda g: ((g // nblk) * n_halo
                           + jnp.minimum((g % nblk + 1) * halo_per_blk, n_halo - 1), 0)
    resident = pl.Buffered(1)
    kernel = functools.partial(_tail_kernel, seq=seq, blocks_per_seq=nblk)
    return pl.pallas_call(
        kernel,
        grid=(n // tq,),
        in_specs=[
            pl.BlockSpec((tq, D_MODEL), tok),
            pl.BlockSpec((HEAD_PAIRS, tq, LANES), lambda g: (0, g, 0)),
            pl.BlockSpec((tq, POOL_WIDTH), tok),
            pl.BlockSpec((POOL_HALO, POOL_WIDTH), prev_halo),
            pl.BlockSpec((POOL_HALO, POOL_WIDTH), next_halo),
            pl.BlockSpec((POOL_WIDTH, POOL_WIDTH), const, pipeline_mode=resident),
            pl.BlockSpec((1, POOL_WIDTH), const),
            pl.BlockSpec((D_MODEL, D_MODEL), const, pipeline_mode=resident),
            pl.BlockSpec((1, D_MODEL), const),
            pl.BlockSpec((D_MODEL, d_ff), const, pipeline_mode=resident),
            pl.BlockSpec((D_MODEL, d_ff), const, pipeline_mode=resident),
            pl.BlockSpec((d_ff, D_MODEL), const, pipeline_mode=resident),
        ],
        out_specs=pl.BlockSpec((tq, D_MODEL), tok),
        out_shape=jax.ShapeDtypeStruct(x2.shape, F32),
        compiler_params=pltpu.CompilerParams(
            dimension_semantics=("arbitrary",), vmem_limit_bytes=VMEM_LIMIT_BYTES),
        name="tail",
    )(x2, a, u, u, u, wpool_bd, pscale, w_out, g2, w_gate, w_up, w_down)


def _bias_table(rpb):
    c = np.arange(GRID_W)
    col_start = np.clip(c - WIN_COLS // 2, 0, GRID_W - WIN_COLS)
    kc = np.arange(2 * GRID_W) % GRID_W
    inside = (kc[None, :] >= col_start[:, None]) & (kc[None, :] < col_start[:, None] + WIN_COLS)
    col_rel = kc[None, :] - c[:, None] + WIN_COLS - 1
    onehot = (col_rel[:, :, None] == np.arange(2 * WIN_COLS - 1)) & inside[:, :, None]
    both = jnp.einsum('hrj,cmj->hrcm', rpb.astype(F32), jnp.asarray(onehot, F32),
                      precision=lax.Precision.HIGHEST)
    low_lanes = np.arange(2 * GRID_W) < GRID_W
    tbl = jnp.where(low_lanes, both[:, :-1], both[:, 1:])
    return jnp.where(inside[None, None], tbl, MASK_VALUE)


def _block_diag(w_pool):
    g, d, _ = w_pool.shape
    eye = jnp.eye(g, dtype=w_pool.dtype)
    return jnp.einsum('gcd,gh->gchd', w_pool, eye).reshape(g * d, g * d)


def kernel(x, norm1_g, w_in, q_norm_g, k_norm_g, rpb, w_pool, pool_scale, w_out,
           norm2_g, w_gate, w_up, w_down):
    batch, seq, _ = x.shape
    depth = w_in.shape[0]
    x2 = x.reshape(batch * seq, D_MODEL)
    for l in range(depth):
        gq = jnp.tile(q_norm_g[l], NA_HEADS)[None, :]
        gk = jnp.tile(k_norm_g[l], NA_HEADS)[None, :]
        (q, k, v, u), (w_out_bf, w_gate_bf, w_up_bf, w_down_bf) = _proj_call(
            x2, norm1_g[l][None, :], w_in, l, gq, gk, (w_out, w_gate, w_up, w_down))
        a = _attn_call(q, k, v, _bias_table(rpb[l] * LOG2_E), batch, seq)
        x2 = _tail_call(x2, a, u, _block_diag(w_pool[l]).astype(BF16), pool_scale[l][None, :],
                        w_out_bf, norm2_g[l][None, :], w_gate_bf, w_up_bf, w_down_bf, seq)
    return x2.reshape(batch, seq, D_MODEL)
```

```python
import functools

import jax
import jax.numpy as jnp
import numpy as np
from jax import lax
from jax.experimental import pallas as pl
from jax.experimental.pallas import tpu as pltpu

D_MODEL = 1024
GRID_W = 64
NA_WIDTH = D_MODEL // 2
NA_HEADS = 8
NA_HEAD_DIM = NA_WIDTH // NA_HEADS
WIN_ROWS_MAX = 8
WIN_COLS = 16
POOL_WIDTH = D_MODEL - NA_WIDTH
POOL_WINDOWS = (2, 4, 8, 16)
POOL_GROUPS = len(POOL_WINDOWS)
POOL_GROUP_DIM = POOL_WIDTH // POOL_GROUPS
IN_WIDTH = 3 * NA_WIDTH + POOL_WIDTH
EPS = 1e-6

LANES = 128
HEAD_PAIRS = NA_WIDTH // LANES
MASK_VALUE = -1e30
LOG2_E = 1.4426950408889634
SCORE_SCALE = NA_HEAD_DIM ** -0.5 * LOG2_E
POOL_HALO = 16
VMEM_LIMIT_BYTES = 56 * 1024 * 1024

PROJ_BLOCK = 1024
N_LATER_WEIGHTS = 4
CAST_BANDS = 16
BF16_SUBLANES = 16
ATTN_ROWS_PER_STEP = 32
ATTN_LOOKAHEAD = 2
FFN_CHUNK = 1024
TAIL_BLOCK = 1024

F32 = jnp.float32
BF16 = jnp.bfloat16


def _rms_scale(x):
    return lax.rsqrt(jnp.mean(x * x, axis=-1, keepdims=True) + EPS)


def _proj_kernel(x_ref, g1_ref, w_ref, gq_ref, gk_ref, *refs):
    later_f32 = refs[:N_LATER_WEIGHTS]
    q_ref, k_ref, v_ref, u_ref = refs[N_LATER_WEIGHTS:N_LATER_WEIGHTS + 4]
    later_bf = refs[N_LATER_WEIGHTS + 4:2 * N_LATER_WEIGHTS + 4]
    w_bf = refs[-1]

    @pl.when(pl.program_id(0) == 0)
    def _():
        w_bf[...] = w_ref[...].astype(BF16)

    for src, dst in zip(later_f32, later_bf):
        dst[...] = src[...].astype(BF16)

    x = x_ref[...]
    h = (x * _rms_scale(x) * g1_ref[...]).astype(BF16)
    proj = jnp.dot(h, w_bf[...], preferred_element_type=F32)
    first_head = lax.broadcasted_iota(jnp.int32, (1, LANES), 1) < NA_HEAD_DIM

    def head_norm(t, g):
        tiles = []
        for p in range(HEAD_PAIRS):
            tp = t[:, p * LANES:(p + 1) * LANES]
            sq = tp * tp
            ss_first = jnp.sum(jnp.where(first_head, sq, 0.0), axis=-1, keepdims=True)
            ss_second = jnp.sum(jnp.where(first_head, 0.0, sq), axis=-1, keepdims=True)
            ms = jnp.where(first_head, ss_first, ss_second) * (1.0 / NA_HEAD_DIM)
            tiles.append(tp * lax.rsqrt(ms + EPS))
        return jnp.concatenate(tiles, axis=-1) * g

    q = head_norm(proj[:, :NA_WIDTH], gq_ref[...])
    k = head_norm(proj[:, NA_WIDTH:2 * NA_WIDTH], gk_ref[...])
    q = (q * SCORE_SCALE).astype(BF16)
    k = k.astype(BF16)
    v = proj[:, 2 * NA_WIDTH:3 * NA_WIDTH].astype(BF16)
    for p in range(HEAD_PAIRS):
        lanes = slice(p * LANES, (p + 1) * LANES)
        q_ref[p] = q[:, lanes]
        k_ref[p] = k[:, lanes]
        v_ref[p] = v[:, lanes]
    u_ref[...] = proj[:, 3 * NA_WIDTH:]


def _proj_call(x2, g1, w_in, layer, gq, gk, later_weights):
    assert len(later_weights) == N_LATER_WEIGHTS
    n = x2.shape[0]
    tm = PROJ_BLOCK
    steps = n // tm
    assert steps % CAST_BANDS == 0
    steps_per_band = steps // CAST_BANDS
    const = lambda i: (0, 0)
    row = lambda i: (i, 0)
    pair_row = lambda i: (0, i, 0)
    band_in, band_out, band_shape = [], [], []
    for w in later_weights:
        rows, cols = w.shape[1:]
        assert rows % (CAST_BANDS * BF16_SUBLANES) == 0
        band_in.append(pl.BlockSpec((None, rows // CAST_BANDS, cols),
                                    lambda i: (layer, i // steps_per_band, 0)))
        band_out.append(pl.BlockSpec((rows // CAST_BANDS, cols), lambda i: (i // steps_per_band, 0)))
        band_shape.append(jax.ShapeDtypeStruct((rows, cols), BF16))
    outs = pl.pallas_call(
        _proj_kernel,
        grid=(steps,),
        in_specs=[
            pl.BlockSpec((tm, D_MODEL), row),
            pl.BlockSpec((1, D_MODEL), const),
            pl.BlockSpec((None, D_MODEL, IN_WIDTH), lambda i: (layer, 0, 0),
                         pipeline_mode=pl.Buffered(1)),
            pl.BlockSpec((1, NA_WIDTH), const),
            pl.BlockSpec((1, NA_WIDTH), const),
        ] + band_in,
        out_specs=[
            pl.BlockSpec((HEAD_PAIRS, tm, LANES), pair_row),
            pl.BlockSpec((HEAD_PAIRS, tm, LANES), pair_row),
            pl.BlockSpec((HEAD_PAIRS, tm, LANES), pair_row),
            pl.BlockSpec((tm, POOL_WIDTH), row),
        ] + band_out,
        out_shape=[
            jax.ShapeDtypeStruct((HEAD_PAIRS, n, LANES), BF16),
            jax.ShapeDtypeStruct((HEAD_PAIRS, n, LANES), BF16),
            jax.ShapeDtypeStruct((HEAD_PAIRS, n, LANES), BF16),
            jax.ShapeDtypeStruct((n, POOL_WIDTH), F32),
        ] + band_shape,
        scratch_shapes=[pltpu.VMEM((D_MODEL, IN_WIDTH), BF16)],
        compiler_params=pltpu.CompilerParams(
            dimension_semantics=("arbitrary",), vmem_limit_bytes=VMEM_LIMIT_BYTES),
        name="proj",
    )(x2, g1, w_in, gq, gk, *later_weights)
    return outs[:4], outs[4:]


def _attn_kernel(q_ref, k_ref, v_ref, bias_ref, a_ref, *, rows):
    wr = WIN_ROWS_MAX
    nkeys = wr * GRID_W
    lane = lax.broadcasted_iota(jnp.int32, (GRID_W, LANES), 1)
    first_head = lane < NA_HEAD_DIM

    def window_start(r):
        return jnp.clip(r - wr // 2, 0, rows - wr)

    def probabilities(r, pair):
        rs = window_start(r)
        first_rel = rs - r + WIN_ROWS_MAX - 1
        q0 = pl.multiple_of(r * GRID_W, GRID_W)
        k0 = pl.multiple_of(rs * GRID_W, GRID_W)
        qs = q_ref[pair, pl.ds(q0, GRID_W), :]
        zero = jnp.zeros_like(qs)
        lhs = jnp.concatenate([jnp.where(first_head, qs, zero),
                               jnp.where(first_head, zero, qs)], axis=0)
        kw = k_ref[pair, pl.ds(k0, nkeys), :]
        s = lax.dot_general(lhs, kw, (((1,), (1,)), ((), ())),
                            preferred_element_type=F32)
        s = s + jnp.concatenate(
            [jnp.concatenate([bias_ref[2 * pair + hh, first_rel + 2 * j] for hh in range(2)], axis=0)
             for j in range(nkeys // LANES)], axis=1)
        m = jnp.max(s, axis=-1, keepdims=True)
        return jnp.exp2(s - m).astype(BF16)

    ones = jnp.ones((nkeys, LANES), BF16)

    def weighted_values(r, pair, p):
        q0 = pl.multiple_of(r * GRID_W, GRID_W)
        k0 = pl.multiple_of(window_start(r) * GRID_W, GRID_W)
        vw = jnp.concatenate([v_ref[pair, pl.ds(k0, nkeys), :], ones], axis=1)
        ol = jnp.dot(p, vw, preferred_element_type=F32)
        merged = jnp.where(jnp.concatenate([first_head, first_head], axis=1),
                           ol[:GRID_W], ol[GRID_W:])
        a_ref[pair, pl.ds(q0, GRID_W), :] = (merged[:, :LANES] / merged[:, LANES:]).astype(BF16)

    def row_group(j, carry):
        tiles = [(ATTN_ROWS_PER_STEP * j + k, pair)
                 for k in range(ATTN_ROWS_PER_STEP) for pair in range(HEAD_PAIRS)]
        ahead = [probabilities(*tiles[t]) for t in range(ATTN_LOOKAHEAD)]
        for t, tile in enumerate(tiles):
            if t + ATTN_LOOKAHEAD < len(tiles):
                ahead.append(probabilities(*tiles[t + ATTN_LOOKAHEAD]))
            weighted_values(*tile, ahead.pop(0))
        return carry

    lax.fori_loop(0, rows // ATTN_ROWS_PER_STEP, row_group, 0)


def _attn_call(q, k, v, bias, batch, seq):
    rows = seq // GRID_W
    assert rows >= WIN_ROWS_MAX and rows % ATTN_ROWS_PER_STEP == 0
    per_batch = lambda b: (0, b, 0)
    blk = pl.BlockSpec((HEAD_PAIRS, seq, LANES), per_batch)
    kernel = functools.partial(_attn_kernel, rows=rows)
    return pl.pallas_call(
        kernel,
        grid=(batch,),
        in_specs=[blk, blk, blk,
                  pl.BlockSpec(bias.shape, lambda b: (0, 0, 0, 0), pipeline_mode=pl.Buffered(1))],
        out_specs=blk,
        out_shape=jax.ShapeDtypeStruct(q.shape, BF16),
        compiler_params=pltpu.CompilerParams(
            dimension_semantics=("arbitrary",), vmem_limit_bytes=VMEM_LIMIT_BYTES),
        name="attn",
    )(q, k, v, bias)


def _pool_minus_token(ext, first_token, seq):
    n = ext.shape[0]
    half = POOL_HALO // 2
    tq = n - 2 * half

    def ahead(e, k):
        return pltpu.roll(e, n - k, axis=0)

    def window_count(t, w):
        return (jnp.minimum(t + w // 2, seq) - jnp.maximum(t - w // 2, 0)).astype(F32)

    edge = lax.broadcasted_iota(jnp.int32, (half, 1), 0)
    t_top = first_token + edge
    t_bot = first_token + tq - half + edge
    out = []
    for g, w in enumerate(POOL_WINDOWS):
        e = ext[:, g * POOL_GROUP_DIM:(g + 1) * POOL_GROUP_DIM]
        tok = e[half:half + tq]
        span = 1
        while 2 * span < w:
            e = e + ahead(e, span)
            span *= 2
        lo = half - w // 2
        first = ahead(e, lo)[:tq] if lo else e[:tq]
        s = first + e[half:half + tq]
        mean = jnp.concatenate([s[:half] / window_count(t_top, w),
                                s[half:tq - half] * (1.0 / w),
                                s[tq - half:] / window_count(t_bot, w)], axis=0)
        out.append(mean - tok)
    return jnp.concatenate(out, axis=-1)


def _tail_kernel(x_ref, a_ref, u_ref, up_ref, un_ref, wpool_ref, pscale_ref, wout_ref,
                 g2_ref, wg_ref, wu_ref, wd_ref, o_ref, *, seq, blocks_per_seq):
    i = pl.program_id(0) % blocks_per_seq
    tq = x_ref.shape[0]
    half = POOL_HALO // 2

    attn = jnp.concatenate([a_ref[p] for p in range(HEAD_PAIRS)], axis=-1)
    x1 = x_ref[...] + jnp.dot(attn, wout_ref[:NA_WIDTH, :], preferred_element_type=F32)

    prev_ok = (i > 0).astype(F32)
    next_ok = (i < blocks_per_seq - 1).astype(F32)
    ext = jnp.concatenate([up_ref[half:, :] * prev_ok, u_ref[...],
                           un_ref[:half, :] * next_ok], axis=0)
    d = _pool_minus_token(ext, i * tq, seq).astype(BF16)
    y = jnp.dot(d, wpool_ref[...], preferred_element_type=F32) * pscale_ref[...]
    x1 = x1 + jnp.dot(y.astype(BF16), wout_ref[NA_WIDTH:, :], preferred_element_type=F32)

    h = (x1 * _rms_scale(x1) * g2_ref[...]).astype(BF16)
    d_ff = wd_ref.shape[0]
    chunks = [(f0, min(f0 + FFN_CHUNK, d_ff)) for f0 in range(0, d_ff, FFN_CHUNK)]

    def gate_up(f0, f1):
        return (jnp.dot(h, wg_ref[:, f0:f1], preferred_element_type=F32),
                jnp.dot(h, wu_ref[:, f0:f1], preferred_element_type=F32))

    ffn = None
    ahead = gate_up(*chunks[0])
    for c, (f0, f1) in enumerate(chunks):
        gate, up = ahead
        if c + 1 < len(chunks):
            ahead = gate_up(*chunks[c + 1])
        act = (gate * jax.nn.sigmoid(gate) * up).astype(BF16)
        part = jnp.dot(act, wd_ref[f0:f1, :], preferred_element_type=F32)
        ffn = part if ffn is None else ffn + part
    o_ref[...] = x1 + ffn


def _tail_call(x2, a, u, wpool_bd, pscale, w_out, g2, w_gate, w_up, w_down, seq):
    n = x2.shape[0]
    tq = TAIL_BLOCK
    d_ff = w_gate.shape[1]
    nblk = seq // tq
    halo_per_blk = tq // POOL_HALO
    n_halo = seq // POOL_HALO
    tok = lambda g: (g, 0)
    const = lambda g: (0, 0)
    prev_halo = lambda g: ((g // nblk) * n_halo + jnp.maximum((g % nblk) * halo_per_blk - 1, 0), 0)
    next_halo = lambda g: ((g // nblk) * n_halo
                           + jnp.minimum((g % nblk + 1) * halo_per_blk, n_halo - 1), 0)
    resident = pl.Buffered(1)
    kernel = functools.partial(_tail_kernel, seq=seq, blocks_per_seq=nblk)
    return pl.pallas_call(
        kernel,
        grid=(n // tq,),
        in_specs=[
            pl.BlockSpec((tq, D_MODEL), tok),
            pl.BlockSpec((HEAD_PAIRS, tq, LANES), lambda g: (0, g, 0)),
            pl.BlockSpec((tq, POOL_WIDTH), tok),
            pl.BlockSpec((POOL_HALO, POOL_WIDTH), prev_halo),
            pl.BlockSpec((POOL_HALO, POOL_WIDTH), next_halo),
            pl.BlockSpec((POOL_WIDTH, POOL_WIDTH), const, pipeline_mode=resident),
            pl.BlockSpec((1, POOL_WIDTH), const),
            pl.BlockSpec((D_MODEL, D_MODEL), const, pipeline_mode=resident),
            pl.BlockSpec((1, D_MODEL), const),
            pl.BlockSpec((D_MODEL, d_ff), const, pipeline_mode=resident),
            pl.BlockSpec((D_MODEL, d_ff), const, pipeline_mode=resident),
            pl.BlockSpec((d_ff, D_MODEL), const, pipeline_mode=resident),
        ],
        out_specs=pl.BlockSpec((tq, D_MODEL), tok),
        out_shape=jax.ShapeDtypeStruct(x2.shape, F32),
        compiler_params=pltpu.CompilerParams(
            dimension_semantics=("arbitrary",), vmem_limit_bytes=VMEM_LIMIT_BYTES),
        name="tail",
    )(x2, a, u, u, u, wpool_bd, pscale, w_out, g2, w_gate, w_up, w_down)


def _bias_table(rpb):
    c = np.arange(GRID_W)
    col_start = np.clip(c - WIN_COLS // 2, 0, GRID_W - WIN_COLS)
    kc = np.arange(2 * GRID_W) % GRID_W
    inside = (kc[None, :] >= col_start[:, None]) & (kc[None, :] < col_start[:, None] + WIN_COLS)
    col_rel = kc[None, :] - c[:, None] + WIN_COLS - 1
    onehot = (col_rel[:, :, None] == np.arange(2 * WIN_COLS - 1)) & inside[:, :, None]
    both = jnp.einsum('hrj,cmj->hrcm', rpb.astype(F32), jnp.asarray(onehot, F32),
                      precision=lax.Precision.HIGHEST)
    low_lanes = np.arange(2 * GRID_W) < GRID_W
    tbl = jnp.where(low_lanes, both[:, :-1], both[:, 1:])
    return jnp.where(inside[None, None], tbl, MASK_VALUE)


def _block_diag(w_pool):
    g, d, _ = w_pool.shape
    eye = jnp.eye(g, dtype=w_pool.dtype)
    return jnp.einsum('gcd,gh->gchd', w_pool, eye).reshape(g * d, g * d)


def kernel(x, norm1_g, w_in, q_norm_g, k_norm_g, rpb, w_pool, pool_scale, w_out,
           norm2_g, w_gate, w_up, w_down):
    batch, seq, _ = x.shape
    depth = w_in.shape[0]
    x2 = x.reshape(batch * seq, D_MODEL)
    for l in range(depth):
        gq = jnp.tile(q_norm_g[l], NA_HEADS)[None, :]
        gk = jnp.tile(k_norm_g[l], NA_HEADS)[None, :]
        (q, k, v, u), (w_out_bf, w_gate_bf, w_up_bf, w_down_bf) = _proj_call(
            x2, norm1_g[l][None, :], w_in, l, gq, gk, (w_out, w_gate, w_up, w_down))
        a = _attn_call(q, k, v, _bias_table(rpb[l] * LOG2_E), batch, seq)
        x2 = _tail_call(x2, a, u, _block_diag(w_pool[l]).astype(BF16), pool_scale[l][None, :],
                        w_out_bf, norm2_g[l][None, :], w_gate_bf, w_up_bf, w_down_bf, seq)
    return x2.reshape(batch, seq, D_MODEL)
```
